```python
import math
import jax
import jax.numpy as jnp
from jax import lax
import numpy as np

D_MODEL = 1024
BATCH = 8
SEQ = 4096
DEPTH = 2


N_EVEN = (DEPTH + 1) // 2
N_ODD = DEPTH // 2
EPS = 1e-6

HG_HEADS = 4
HG_DIM = 128
HG_WIDTH = HG_HEADS * HG_DIM
HG_CHUNK = 64

MLA_HEADS = 4
MLA_Q_RANK = 256
MLA_KV_RANK = 128
MLA_NOPE = 128
MLA_ROPE = 64
MLA_V = 128
MLA_QK = MLA_NOPE + MLA_ROPE
MLA_WIDTH = MLA_HEADS * MLA_V
ATTN_BLOCK = 128
ROPE_BASE = 10000.0

IN_SPLITS = (HG_WIDTH, HG_WIDTH, HG_WIDTH, HG_WIDTH, MLA_Q_RANK, MLA_KV_RANK, MLA_ROPE)
IN_WIDTH = 4 * HG_WIDTH + MLA_Q_RANK + MLA_KV_RANK + MLA_ROPE
MIX_WIDTH = HG_WIDTH + MLA_WIDTH

S5_GROUP = 16
S5_GROUPS = D_MODEL // S5_GROUP
S5_STATE = 64
S5_CHUNK = 128
DT_MIN = 1e-3
DT_MAX = 1e-1

MEM_LEN = 256
XA_HEADS = 4
XA_DIM = D_MODEL // XA_HEADS

D_FF = 2816
CONV_W = 3

kernel_name = 'hybrid_hgrn2_mla_s5_memxattn_convffn'


def rms_norm(x, gain):
    xf = x.astype(jnp.float32)
    y = xf * lax.rsqrt(jnp.mean(xf * xf, axis=-1, keepdims=True) + EPS)
    return (y * gain.astype(jnp.float32)).astype(x.dtype)


def rope_tables(positions):
    inv_freq = 1.0 / (ROPE_BASE ** (jnp.arange(0, MLA_ROPE, 2, dtype=jnp.float32) / MLA_ROPE))
    ang = positions.astype(jnp.float32)[..., None] * inv_freq
    return jnp.cos(ang), jnp.sin(ang)


def apply_rope(x, cos, sin):
    xf = x.astype(jnp.float32)
    x1, x2 = jnp.split(xf, 2, axis=-1)
    c = cos[:, :, None, :]
    s = sin[:, :, None, :]
    return jnp.concatenate([x1 * c - x2 * s, x1 * s + x2 * c], axis=-1).astype(x.dtype)


def hgrn2_recurrence(q, f_logit, i_val, lb):
    bsz, seq, heads, dh = q.shape
    nc = seq // HG_CHUNK
    lbh = lb.astype(jnp.float32).reshape(heads, dh)
    f = lbh + (1.0 - lbh) * jax.nn.sigmoid(f_logit.astype(jnp.float32))
    log_f = jnp.log(f)
    k = 1.0 - f
    qf = jax.nn.silu(q.astype(jnp.float32))
    v = i_val.astype(jnp.float32)

    def to_chunks(t):
        return t.reshape(bsz, nc, HG_CHUNK, heads, dh).transpose(1, 0, 3, 2, 4)

    qc, kc, vc = to_chunks(qf), to_chunks(k), to_chunks(v)
    bc = jnp.cumsum(to_chunks(log_f), axis=3)
    causal = jnp.tril(jnp.ones((HG_CHUNK, HG_CHUNK), dtype=bool))[:, :, None]

    def chunk_step(state, inp):
        qt, kt, vt, bt = inp
        diff = bt[:, :, :, None, :] - bt[:, :, None, :, :]
        decay = jnp.exp(jnp.where(causal, diff, -jnp.inf))
        scores = jnp.einsum('bhtk,bhsk,bhtsk->bhts', qt, kt, decay)
        out = (jnp.einsum('bhts,bhsv->bhtv', scores, vt)
               + jnp.einsum('bhtk,bhkv->bhtv', qt * jnp.exp(bt), state))
        b_end = bt[:, :, -1:, :]
        new_state = (jnp.exp(b_end[:, :, 0, :])[..., None] * state
                     + jnp.einsum('bhsk,bhsv->bhkv', kt * jnp.exp(b_end - bt), vt))
        return new_state, out

    state0 = jnp.zeros((bsz, heads, dh, dh), jnp.float32)
    _, out = lax.scan(chunk_step, state0, (qc, kc, vc, bc))
    return out.transpose(1, 0, 3, 2, 4).reshape(bsz, seq, heads, dh)


def causal_block_attention(q, k, v, scale):
    bsz, seq, heads, dk = q.shape
    nb = seq // ATTN_BLOCK
    q_blocks = q.reshape(bsz, nb, ATTN_BLOCK, heads, dk).transpose(1, 0, 2, 3, 4)
    k_pos = jnp.arange(seq)

    def one_block(args):
        qb, b_idx = args
        s = jnp.einsum('bqhd,bkhd->bhqk', qb, k, preferred_element_type=jnp.float32) * scale
        q_pos = b_idx * ATTN_BLOCK + jnp.arange(ATTN_BLOCK)
        s = jnp.where(k_pos[None, :] <= q_pos[:, None], s, -jnp.inf)
        p = jax.nn.softmax(s, axis=-1).astype(v.dtype)
        return jnp.einsum('bhqk,bkhd->bqhd', p, v)

    out = lax.map(one_block, (q_blocks, jnp.arange(nb)))
    return out.transpose(1, 0, 2, 3, 4).reshape(bsz, seq, heads, v.shape[-1])


def hgrn2_mla_mixer(h, cos, sin, lb, w_in, hg_out_norm, q_a_norm, w_uq, kv_a_norm, w_ukv,
                    qn_nope, qn_rope, kn_nope, kn_rope, w_out):
    bsz, seq, _ = h.shape
    split_at = np.cumsum(IN_SPLITS)[:-1].tolist()
    q_hg, f_hg, i_hg, g_hg, c_q, c_kv, k_pe = jnp.split(h @ w_in, split_at, axis=-1)

    def as_heads(t):
        return t.reshape(bsz, seq, HG_HEADS, HG_DIM)
    o_hg = hgrn2_recurrence(as_heads(q_hg), as_heads(f_hg), as_heads(i_hg), lb).astype(h.dtype)
    o_hg = rms_norm(o_hg, hg_out_norm.reshape(HG_HEADS, HG_DIM)).reshape(bsz, seq, HG_WIDTH)
    o_hg = o_hg * jax.nn.silu(g_hg)

    q = (rms_norm(c_q, q_a_norm) @ w_uq).reshape(bsz, seq, MLA_HEADS, MLA_QK)
    kv = (rms_norm(c_kv, kv_a_norm) @ w_ukv).reshape(bsz, seq, MLA_HEADS, MLA_NOPE + MLA_V)
    q_nope = rms_norm(q[..., :MLA_NOPE], qn_nope)
    q_pe = apply_rope(rms_norm(q[..., MLA_NOPE:], qn_rope), cos, sin)
    k_nope = rms_norm(kv[..., :MLA_NOPE], kn_nope)
    v = kv[..., MLA_NOPE:]
    k_pe = apply_rope(rms_norm(k_pe, kn_rope)[:, :, None, :], cos, sin)
    k_pe = jnp.broadcast_to(k_pe, (bsz, seq, MLA_HEADS, MLA_ROPE))
    q_full = jnp.concatenate([q_nope, q_pe], axis=-1)
    k_full = jnp.concatenate([k_nope, k_pe], axis=-1)
    o_mla = causal_block_attention(q_full, k_full, v, MLA_QK ** -0.5).reshape(bsz, seq, MLA_WIDTH)

    return jnp.concatenate([o_hg, o_mla], axis=-1) @ w_out


def complex_affine_combine(e1, e2):
    a1r, a1i, b1r, b1i = e1
    a2r, a2i, b2r, b2i = e2
    ar = a2r * a1r - a2i * a1i
    ai = a2r * a1i + a2i * a1r
    br = a2r * b1r - a2i * b1i + b2r
    bi = a2r * b1i + a2i * b1r + b2i
    return ar, ai, br, bi


def s5_mixer(u, lam_re, lam_im, log_dt, b_re, b_im, c_re, c_im, d_skip, w_glu_a, w_glu_b):
    bsz, seq, _ = u.shape
    nc = seq // S5_CHUNK
    f32 = jnp.float32
    lr = lam_re.astype(f32)
    li = lam_im.astype(f32)
    dt = jnp.exp(log_dt.astype(f32))[:, None]
    mag = jnp.exp(lr * dt)
    ab_re = mag * jnp.cos(li * dt)
    ab_im = mag * jnp.sin(li * dt)
    den = lr * lr + li * li
    z_re = ((ab_re - 1.0) * lr + ab_im * li) / den
    z_im = (ab_im * lr - (ab_re - 1.0) * li) / den
    br = b_re.astype(f32)
    bi = b_im.astype(f32)
    bb_re = z_re[..., None] * br - z_im[..., None] * bi
    bb_im = z_re[..., None] * bi + z_im[..., None] * br
    cr = c_re.astype(f32)
    ci = c_im.astype(f32)
    uf = u.astype(f32)
    u_chunks = uf.reshape(bsz, nc, S5_CHUNK, S5_GROUPS, S5_GROUP).transpose(1, 0, 2, 3, 4)

    def chunk_step(carry, uc):
        hr0, hi0 = carry
        bu_re = jnp.einsum('bcgm,gpm->bcgp', uc, bb_re)
        bu_im = jnp.einsum('bcgm,gpm->bcgp', uc, bb_im)
        a_re = jnp.broadcast_to(ab_re, bu_re.shape)
        a_im = jnp.broadcast_to(ab_im, bu_im.shape)
        p_re, p_im, s_re, s_im = lax.associative_scan(
            complex_affine_combine, (a_re, a_im, bu_re, bu_im), axis=1)
        hr = s_re + p_re * hr0[:, None] - p_im * hi0[:, None]
        hi = s_im + p_re * hi0[:, None] + p_im * hr0[:, None]
        y = jnp.einsum('bcgp,gmp->bcgm', hr, cr) - jnp.einsum('bcgp,gmp->bcgm', hi, ci)
        return (hr[:, -1], hi[:, -1]), y

    init = (jnp.zeros((bsz, S5_GROUPS, S5_STATE), f32), jnp.zeros((bsz, S5_GROUPS, S5_STATE), f32))
    _, y = lax.scan(chunk_step, init, u_chunks)
    y = y.transpose(1, 0, 2, 3, 4).reshape(bsz, seq, D_MODEL) + d_skip.astype(f32) * uf
    y = jax.nn.gelu(y).astype(u.dtype)
    return (y @ w_glu_a) * jax.nn.sigmoid(y @ w_glu_b)


def memory_cross_attention(h, mem, mem_gain, wq, wk, wv, wo, q_gain, k_gain):
    bsz, seq, _ = h.shape
    m_len = mem.shape[1]
    m = rms_norm(mem, mem_gain)
    q = rms_norm((h @ wq).reshape(bsz, seq, XA_HEADS, XA_DIM), q_gain)
    k = rms_norm((m @ wk).reshape(bsz, m_len, XA_HEADS, XA_DIM), k_gain)
    v = (m @ wv).reshape(bsz, m_len, XA_HEADS, XA_DIM)
    s = jnp.einsum('bqhd,bkhd->bhqk', q, k, preferred_element_type=jnp.float32) * (XA_DIM ** -0.5)
    p = jax.nn.softmax(s, axis=-1).astype(v.dtype)
    o = jnp.einsum('bhqk,bkhd->bqhd', p, v).reshape(bsz, seq, D_MODEL)
    return o @ wo


def conv_gated_mlp(h, w_up, conv_w, conv_b, w_down):
    u = h @ w_up
    u = lax.conv_general_dilated(
        u, conv_w[:, None, :].astype(u.dtype), window_strides=(1,), padding=[(CONV_W - 1, 0)],
        dimension_numbers=('NWC', 'WIO', 'NWC'), feature_group_count=2 * D_FF) + conv_b
    gate, val = jnp.split(u, 2, axis=-1)
    return (jax.nn.silu(gate) * val) @ w_down


def setup_inputs(seed: int = 0) -> dict:
    key = jax.random.key(seed)
    keys = jax.random.split(key, 48)
    counter = [0]

    def nxt():
        kk = keys[counter[0]]
        counter[0] += 1
        return kk

    def nrm(shape, scale):
        return scale * jax.random.normal(nxt(), shape, jnp.float32)

    def gain(shape):
        return 1.0 + nrm(shape, 0.02)

    d = D_MODEL
    inp = {}
    inp['x'] = nrm((BATCH, SEQ, d), 1.0)
    inp['mem'] = nrm((BATCH, MEM_LEN, d), 1.0)
    inp['positions'] = (jnp.arange(SEQ, dtype=jnp.int32)[None, :]
                        + jax.random.randint(nxt(), (BATCH, 1), 0, 1024, dtype=jnp.int32))
    inp['norm_mix'] = gain((DEPTH, d))
    inp['norm_xa'] = gain((DEPTH, d))
    inp['norm_mem'] = gain((DEPTH, d))
    inp['norm_ffn'] = gain((DEPTH, d))
    inp['xa_wq'] = nrm((DEPTH, d, d), d ** -0.5)
    inp['xa_wk'] = nrm((DEPTH, d, d), d ** -0.5)
    inp['xa_wv'] = nrm((DEPTH, d, d), d ** -0.5)
    inp['xa_wo'] = nrm((DEPTH, d, d), d ** -0.5)
    inp['xa_q_norm'] = gain((DEPTH, XA_DIM))
    inp['xa_k_norm'] = gain((DEPTH, XA_DIM))
    inp['ffn_w_up'] = nrm((DEPTH, d, 2 * D_FF), d ** -0.5)
    inp['ffn_conv_w'] = nrm((DEPTH, CONV_W, 2 * D_FF), CONV_W ** -0.5)
    inp['ffn_conv_b'] = nrm((DEPTH, 2 * D_FF), 0.01)
    inp['ffn_w_down'] = nrm((DEPTH, D_FF, d), D_FF ** -0.5)
    inp['hg_lb_logits'] = nrm((N_EVEN + 1, HG_WIDTH), 0.1)
    inp['mix_w_in'] = nrm((N_EVEN, d, IN_WIDTH), d ** -0.5)
    inp['hg_out_norm'] = gain((N_EVEN, HG_WIDTH))
    inp['mla_q_a_norm'] = gain((N_EVEN, MLA_Q_RANK))
    inp['mla_w_uq'] = nrm((N_EVEN, MLA_Q_RANK, MLA_HEADS * MLA_QK), MLA_Q_RANK ** -0.5)
    inp['mla_kv_a_norm'] = gain((N_EVEN, MLA_KV_RANK))
    inp['mla_w_ukv'] = nrm((N_EVEN, MLA_KV_RANK, MLA_HEADS * (MLA_NOPE + MLA_V)), MLA_KV_RANK ** -0.5)
    inp['mla_qn_nope'] = gain((N_EVEN, MLA_NOPE))
    inp['mla_qn_rope'] = gain((N_EVEN, MLA_ROPE))
    inp['mla_kn_nope'] = gain((N_EVEN, MLA_NOPE))
    inp['mla_kn_rope'] = gain((N_EVEN, MLA_ROPE))
    inp['mix_w_out'] = nrm((N_EVEN, MIX_WIDTH, d), MIX_WIDTH ** -0.5)
    inp['s5_lam_re'] = -0.5 + nrm((N_ODD, S5_GROUPS, S5_STATE), 0.01)
    inp['s5_lam_im'] = (math.pi * jnp.arange(S5_STATE, dtype=jnp.float32)[None, None, :]
                        + nrm((N_ODD, S5_GROUPS, S5_STATE), 0.01))
    inp['s5_log_dt'] = jax.random.uniform(nxt(), (N_ODD, S5_GROUPS), jnp.float32,
                                          math.log(DT_MIN), math.log(DT_MAX))
    inp['s5_b_re'] = nrm((N_ODD, S5_GROUPS, S5_STATE, S5_GROUP), (2 * S5_GROUP) ** -0.5)
    inp['s5_b_im'] = nrm((N_ODD, S5_GROUPS, S5_STATE, S5_GROUP), (2 * S5_GROUP) ** -0.5)
    inp['s5_c_re'] = nrm((N_ODD, S5_GROUPS, S5_GROUP, S5_STATE), S5_STATE ** -0.5)
    inp['s5_c_im'] = nrm((N_ODD, S5_GROUPS, S5_GROUP, S5_STATE), S5_STATE ** -0.5)
    inp['s5_d'] = nrm((N_ODD, d), 1.0)
    inp['s5_w_glu_a'] = nrm((N_ODD, d, d), d ** -0.5)
    inp['s5_w_glu_b'] = nrm((N_ODD, d, d), d ** -0.5)
    return inp


def reference(x, mem, positions, norm_mix, norm_xa, norm_mem, norm_ffn,
              xa_wq, xa_wk, xa_wv, xa_wo, xa_q_norm, xa_k_norm,
              ffn_w_up, ffn_conv_w, ffn_conv_b, ffn_w_down,
              hg_lb_logits, mix_w_in, hg_out_norm, mla_q_a_norm, mla_w_uq, mla_kv_a_norm, mla_w_ukv,
              mla_qn_nope, mla_qn_rope, mla_kn_nope, mla_kn_rope, mix_w_out,
              s5_lam_re, s5_lam_im, s5_log_dt, s5_b_re, s5_b_im, s5_c_re, s5_c_im, s5_d,
              s5_w_glu_a, s5_w_glu_b):
    cos, sin = rope_tables(positions)
    lb_all = jnp.cumsum(jax.nn.softmax(hg_lb_logits.astype(jnp.float32), axis=0), axis=0)
    h = x
    for layer in range(DEPTH):
        j = layer // 2
        hn = rms_norm(h, norm_mix[layer])
        if layer % 2 == 0:
            mix = hgrn2_mla_mixer(hn, cos, sin, lb_all[j], mix_w_in[j], hg_out_norm[j],
                                  mla_q_a_norm[j], mla_w_uq[j], mla_kv_a_norm[j], mla_w_ukv[j],
                                  mla_qn_nope[j], mla_qn_rope[j], mla_kn_nope[j], mla_kn_rope[j],
                                  mix_w_out[j])
        else:
            mix = s5_mixer(hn, s5_lam_re[j], s5_lam_im[j], s5_log_dt[j], s5_b_re[j], s5_b_im[j],
                           s5_c_re[j], s5_c_im[j], s5_d[j], s5_w_glu_a[j], s5_w_glu_b[j])
        h = h + mix
        h = h + memory_cross_attention(rms_norm(h, norm_xa[layer]), mem, norm_mem[layer],
                                       xa_wq[layer], xa_wk[layer], xa_wv[layer], xa_wo[layer],
                                       xa_q_norm[layer], xa_k_norm[layer])
        h = h + conv_gated_mlp(rms_norm(h, norm_ffn[layer]), ffn_w_up[layer], ffn_conv_w[layer],
                               ffn_conv_b[layer], ffn_w_down[layer])
    return h
```

```python
import functools

import numpy as np
import jax
import jax.numpy as jnp
from jax import lax
from jax.experimental import pallas as pl
from jax.experimental.pallas import tpu as pltpu

F32 = jnp.float32
BF16 = jnp.bfloat16
EPS = 1e-6
NEG_BIG = -1e30

D_MODEL = 1024
HG_HEADS = 4
HG_DIM = 128
HG_WIDTH = HG_HEADS * HG_DIM
HG_CHUNK = 64
HG_LEVELS = (32, 16, 8, 4, 2, 1)
MLA_HEADS = 4
MLA_Q_RANK = 256
MLA_KV_RANK = 128
MLA_NOPE = 128
MLA_ROPE = 64
MLA_V = 128
MLA_QK = MLA_NOPE + MLA_ROPE
MLA_QK_PAD = 256
ROPE_BASE = 10000.0
S5_GROUP = 16
S5_GROUPS = D_MODEL // S5_GROUP
S5_STATE = 64
S5_BLOCKS = 8
XA_HEADS = 4
XA_DIM = D_MODEL // XA_HEADS
D_FF = 2816
FF_CHUNK = 256
N_FF = D_FF // FF_CHUNK
IN_PAD = 4 * HG_WIDTH + MLA_Q_RANK + MLA_KV_RANK + 128

VMEM_LIMIT_BYTES = 56 * 1024 * 1024


def _cparams(*sem):
    return pltpu.CompilerParams(dimension_semantics=sem, vmem_limit_bytes=VMEM_LIMIT_BYTES)


def _dot(a, b):
    return jnp.dot(a, b, preferred_element_type=F32)


def _dot_nt(a, b):
    return lax.dot_general(a, b, (((1,), (1,)), ((), ())), preferred_element_type=F32)


def _rms(x, gain, width=None):
    n = x.shape[-1] if width is None else width
    ms = jnp.sum(x * x, axis=-1, keepdims=True) * (1.0 / n)
    return x * lax.rsqrt(ms + EPS) * gain


def _sigmoid(x):
    return 1.0 / (1.0 + jnp.exp(-x))


def _silu(x):
    return x * _sigmoid(x)


def _rope_kernel(pos_ref, invf_ref, cos_ref, sin_ref):
    ang = pos_ref[...].astype(F32) * invf_ref[...]
    cos_ref[...] = jnp.cos(ang)
    sin_ref[...] = jnp.sin(ang)


def _rope_tables(positions):
    bsz, seq = positions.shape
    half = MLA_ROPE // 2
    inv_freq = 1.0 / (ROPE_BASE ** (jnp.arange(0, MLA_ROPE, 2, dtype=F32) / MLA_ROPE))
    rep = 128 // half
    rows = bsz * seq // rep
    pos_rep = jnp.repeat(positions.reshape(rows, rep), half, axis=1)
    invf = jnp.tile(inv_freq, rep).reshape(1, 128)
    tr = min(rows, 1024)
    cos, sin = pl.pallas_call(
        _rope_kernel,
        grid=(rows // tr,),
        in_specs=[pl.BlockSpec((tr, 128), lambda i: (i, 0)),
                  pl.BlockSpec((1, 128), lambda i: (0, 0))],
        out_specs=[pl.BlockSpec((tr, 128), lambda i: (i, 0))] * 2,
        out_shape=[jax.ShapeDtypeStruct((rows, 128), F32)] * 2,
        compiler_params=_cparams("parallel"),
    )(pos_rep, invf)
    cos = cos.reshape(bsz, seq, half)
    sin = sin.reshape(bsz, seq, half)
    zeros = jnp.zeros((bsz, seq, 2 * half), F32)
    ctab = jnp.concatenate([cos, cos, zeros], axis=-1)
    stab = jnp.concatenate([-sin, sin, zeros], axis=-1)
    return ctab, stab


def _memkv_kernel(mem_ref, g_ref, wk_ref, wv_ref, kg_ref, k_ref, v_ref):
    m = _rms(mem_ref[0], g_ref[0]).astype(BF16)
    k = _dot(m, wk_ref[0])
    v = _dot(m, wv_ref[0])
    for h in range(XA_HEADS):
        sl = slice(h * XA_DIM, (h + 1) * XA_DIM)
        k_ref[0, 0, :, sl] = _rms(k[:, sl], kg_ref[0]).astype(BF16)
    v_ref[0, 0] = v.astype(BF16)


def _memory_kv(mem, norm_mem, wk, wv, k_gain):
    depth = norm_mem.shape[0]
    bsz, mlen, d = mem.shape
    out = jax.ShapeDtypeStruct((depth, bsz, mlen, d), BF16)
    return pl.pallas_call(
        _memkv_kernel,
        grid=(depth, bsz),
        in_specs=[pl.BlockSpec((1, mlen, d), lambda l, b: (b, 0, 0)),
                  pl.BlockSpec((1, 1, d), lambda l, b: (l, 0, 0)),
                  pl.BlockSpec((1, d, d), lambda l, b: (l, 0, 0)),
                  pl.BlockSpec((1, d, d), lambda l, b: (l, 0, 0)),
                  pl.BlockSpec((1, 1, XA_DIM), lambda l, b: (l, 0, 0))],
        out_specs=[pl.BlockSpec((1, 1, mlen, d), lambda l, b: (l, b, 0, 0))] * 2,
        out_shape=[out, out],
        compiler_params=_cparams("parallel", "parallel"),
    )(mem, norm_mem.reshape(depth, 1, d), wk.astype(BF16), wv.astype(BF16),
      k_gain.reshape(depth, 1, XA_DIM))


def _mix_prep_kernel(h_ref, gain_ref, win_ref, qa_ref, wuq_ref, kva_ref, wukv_ref,
                     qnn_ref, qnr_ref, knn_ref, knr_ref, ctab_ref, stab_ref,
                     hg_ref, q_ref, k_ref, v_ref):
    hn = _rms(h_ref[0], gain_ref[...]).astype(BF16)
    proj = _dot(hn, win_ref[...])
    hg_ref[0] = proj[:, :4 * HG_WIDTH]
    o = 4 * HG_WIDTH
    c_q = proj[:, o:o + MLA_Q_RANK]
    c_kv = proj[:, o + MLA_Q_RANK:o + MLA_Q_RANK + MLA_KV_RANK]
    k_pe = proj[:, o + MLA_Q_RANK + MLA_KV_RANK:]
    q = _dot(_rms(c_q, qa_ref[...]).astype(BF16), wuq_ref[...])
    kv = _dot(_rms(c_kv, kva_ref[...]).astype(BF16), wukv_ref[...])
    ctab = ctab_ref[0]
    stab = stab_ref[0]
    half = MLA_ROPE // 2

    def rope(x):
        swapped = pltpu.roll(x, half, 1) + pltpu.roll(x, 128 - half, 1)
        return x * ctab + swapped * stab

    scale = MLA_QK ** -0.5
    k_rope = rope(_rms(k_pe, knr_ref[...], MLA_ROPE)).astype(BF16)
    for h in range(MLA_HEADS):
        b0 = h * MLA_QK_PAD
        qn = _rms(q[:, b0:b0 + MLA_NOPE], qnn_ref[...])
        qr = rope(_rms(q[:, b0 + MLA_NOPE:b0 + MLA_QK_PAD], qnr_ref[...], MLA_ROPE))
        q_ref[0, h, :, :MLA_NOPE] = (qn * scale).astype(BF16)
        q_ref[0, h, :, MLA_NOPE:] = (qr * scale).astype(BF16)
        kn = _rms(kv[:, b0:b0 + MLA_NOPE], knn_ref[...])
        k_ref[0, h, :, :MLA_NOPE] = kn.astype(BF16)
        k_ref[0, h, :, MLA_NOPE:] = k_rope
        v_ref[0, h] = kv[:, b0 + MLA_NOPE:b0 + MLA_NOPE + MLA_V].astype(BF16)


def _mix_prep(h, gain, w_in, q_a_norm, w_uq, kv_a_norm, w_ukv, qn_nope, qn_rope, kn_nope, kn_rope,
              ctab, stab):
    bsz, seq, d = h.shape
    tl = min(seq, 512)
    w_in_p = jnp.pad(w_in, ((0, 0), (0, IN_PAD - w_in.shape[1]))).astype(BF16)
    w_uq_p = jnp.pad(w_uq.reshape(MLA_Q_RANK, MLA_HEADS, MLA_QK),
                     ((0, 0), (0, 0), (0, MLA_QK_PAD - MLA_QK))).reshape(MLA_Q_RANK, -1).astype(BF16)
    pad64 = lambda g: jnp.pad(g, (0, 128 - MLA_ROPE)).reshape(1, 128)
    row = lambda g: g.reshape(1, -1)
    const = lambda shape: pl.BlockSpec(shape, lambda b, i: (0,) * len(shape))
    hw = MLA_HEADS
    return pl.pallas_call(
        _mix_prep_kernel,
        grid=(bsz, seq // tl),
        in_specs=[pl.BlockSpec((1, tl, d), lambda b, i: (b, i, 0)),
                  const((1, d)), const((d, IN_PAD)),
                  const((1, MLA_Q_RANK)), const((MLA_Q_RANK, hw * MLA_QK_PAD)),
                  const((1, MLA_KV_RANK)), const((MLA_KV_RANK, hw * (MLA_NOPE + MLA_V))),
                  const((1, MLA_NOPE)), const((1, 128)), const((1, MLA_NOPE)), const((1, 128)),
                  pl.BlockSpec((1, tl, 128), lambda b, i: (b, i, 0)),
                  pl.BlockSpec((1, tl, 128), lambda b, i: (b, i, 0))],
        out_specs=[pl.BlockSpec((1, tl, 4 * HG_WIDTH), lambda b, i: (b, i, 0)),
                   pl.BlockSpec((1, hw, tl, MLA_QK_PAD), lambda b, i: (b, 0, i, 0)),
                   pl.BlockSpec((1, hw, tl, MLA_QK_PAD), lambda b, i: (b, 0, i, 0)),
                   pl.BlockSpec((1, hw, tl, MLA_V), lambda b, i: (b, 0, i, 0))],
        out_shape=[jax.ShapeDtypeStruct((bsz, seq, 4 * HG_WIDTH), F32),
                   jax.ShapeDtypeStruct((bsz, hw, seq, MLA_QK_PAD), BF16),
                   jax.ShapeDtypeStruct((bsz, hw, seq, MLA_QK_PAD), BF16),
                   jax.ShapeDtypeStruct((bsz, hw, seq, MLA_V), BF16)],
        compiler_params=_cparams("parallel", "parallel"),
    )(h, row(gain), w_in_p, row(q_a_norm), w_uq_p, row(kv_a_norm), w_ukv.astype(BF16),
      row(qn_nope), pad64(qn_rope), row(kn_nope), pad64(kn_rope), ctab, stab)


def _hgrn_tables():
    c = HG_CHUNK
    blocks = []
    for s in HG_LEVELS:
        up = np.zeros((c, c), np.float32)
        lo = np.zeros((c, c), np.float32)
        for t in range(c):
            mid = (t // (2 * s)) * 2 * s + s
            if t % (2 * s) >= s:
                up[t, mid:t + 1] = 1.0
            else:
                lo[t, t + 1:mid] = 1.0
        blocks += [up, lo]
    blocks.append(np.tril(np.ones((c, c), np.float32)))
    blocks.append(np.triu(np.ones((c, c), np.float32), 1))
    m = np.concatenate(blocks, axis=0)
    return np.concatenate([m, m, m], axis=1)


def _hgrn_kernel(hg_ref, lbl_ref, gn_ref, m3_ref, o_ref, st_ref, *, tb, layer_index):
    c = HG_CHUNK

    @pl.when(pl.program_id(1) == 0)
    def _():
        st_ref[...] = jnp.zeros_like(st_ref)

    lg = lbl_ref[...]
    eg = jnp.exp(lg - jnp.max(lg, axis=0, keepdims=True))
    lb_row = jnp.sum(eg[:layer_index + 1], axis=0, keepdims=True) / jnp.sum(eg, axis=0, keepdims=True)

    ti = lax.broadcasted_iota(jnp.int32, (c, c), 0)
    si = lax.broadcasted_iota(jnp.int32, (c, c), 1)
    xor = ti ^ si
    lower = ti > si
    level_masks = [lower & (xor >= s) & (xor < 2 * s) for s in HG_LEVELS]
    on_diag = ti == si
    m3 = m3_ref[...]

    def chunk(ci, carry):
        r0 = pl.multiple_of(ci * c, c)
        for h in range(HG_HEADS):
            lane = slice(h * HG_DIM, (h + 1) * HG_DIM)
            q = hg_ref[0, pl.ds(r0, c), h * HG_DIM:(h + 1) * HG_DIM]
            fl = hg_ref[0, pl.ds(r0, c), HG_WIDTH + h * HG_DIM:HG_WIDTH + (h + 1) * HG_DIM]
            iv = hg_ref[0, pl.ds(r0, c), 2 * HG_WIDTH + h * HG_DIM:2 * HG_WIDTH + (h + 1) * HG_DIM]
            g = hg_ref[0, pl.ds(r0, c), 3 * HG_WIDTH + h * HG_DIM:3 * HG_WIDTH + (h + 1) * HG_DIM]
            lb = lb_row[:, lane]
            f = lb + (1.0 - lb) * _sigmoid(fl)
            lf = jnp.log(f)
            kk = 1.0 - f
            qf = _silu(q)
            hi = lf.astype(BF16)
            r1 = lf - hi.astype(F32)
            mid = r1.astype(BF16)
            lo = (r1 - mid.astype(F32)).astype(BF16)
            ex = jnp.exp(_dot(m3, jnp.concatenate([hi, mid, lo], axis=0)))
            vb = iv.astype(BF16)
            sc = jnp.where(on_diag, jnp.sum(qf * kk, axis=-1, keepdims=True), 0.0)
            for li in range(len(HG_LEVELS)):
                qs = (qf * ex[2 * li * c:(2 * li + 1) * c]).astype(BF16)
                ks = (kk * ex[(2 * li + 1) * c:(2 * li + 2) * c]).astype(BF16)
                sc = jnp.where(level_masks[li], _dot_nt(qs, ks), sc)
            nl = 2 * len(HG_LEVELS)
            eb = ex[nl * c:(nl + 1) * c]
            ee = ex[(nl + 1) * c:(nl + 2) * c]
            st = st_ref[h]
            o = _dot(sc.astype(BF16), vb) + _dot_nt((qf * eb).astype(BF16), st.astype(BF16))
            kend = (kk * ee).astype(BF16)
            st_ref[h] = st * eb[c - 1:c, :] + _dot(iv.T.astype(BF16), kend)
            on = _rms(o, gn_ref[:, lane]) * _silu(g)
            o_ref[0, pl.ds(r0, c), h * HG_DIM:(h + 1) * HG_DIM] = on.astype(o_ref.dtype)
        return carry

    lax.fori_loop(0, tb // c, chunk, 0)


def _hgrn(hg, lb_logits, layer_index, out_norm):
    bsz, seq, _ = hg.shape
    tb = min(seq, 512)
    m3 = jnp.asarray(_hgrn_tables(), BF16)
    return pl.pallas_call(
        functools.partial(_hgrn_kernel, tb=tb, layer_index=layer_index),
        grid=(bsz, seq // tb),
        in_specs=[pl.BlockSpec((1, tb, 4 * HG_WIDTH), lambda b, i: (b, i, 0)),
                  pl.BlockSpec(lb_logits.shape, lambda b, i: (0, 0)),
                  pl.BlockSpec((1, HG_WIDTH), lambda b, i: (0, 0)),
                  pl.BlockSpec(m3.shape, lambda b, i: (0, 0))],
        out_specs=pl.BlockSpec((1, tb, HG_WIDTH), lambda b, i: (b, i, 0)),
        out_shape=jax.ShapeDtypeStruct((bsz, seq, HG_WIDTH), BF16),
        scratch_shapes=[pltpu.VMEM((HG_HEADS, HG_DIM, HG_DIM), F32)],
        compiler_params=_cparams("parallel", "arbitrary"),
    )(hg, lb_logits, out_norm.reshape(1, HG_WIDTH), m3)


def _mla_attn_kernel(q_ref, k_ref, v_ref, o_ref, *, tq, tk):
    qi = pl.program_id(2)
    q = q_ref[0, 0]
    row = lax.broadcasted_iota(jnp.int32, (tq, tk), 0)
    col = lax.broadcasted_iota(jnp.int32, (tq, tk), 1)
    per_q = tq // tk

    def step(j, carry, diag_index):
        m, l, acc = carry
        k0 = pl.multiple_of(j * tk, tk)
        kj = k_ref[0, 0, pl.ds(k0, tk), :]
        vj = v_ref[0, 0, pl.ds(k0, tk), :]
        s = _dot_nt(q, kj)
        if diag_index is not None:
            s = jnp.where(col + diag_index * tk <= row, s, NEG_BIG)
        m_new = jnp.maximum(m, jnp.max(s, axis=-1, keepdims=True))
        alpha = jnp.exp(m - m_new)
        p = jnp.exp(s - m_new)
        l = alpha * l + jnp.sum(p, axis=-1, keepdims=True)
        acc = alpha * acc + _dot(p.astype(BF16), vj)
        return m_new, l, acc

    carry = (jnp.full((tq, 1), NEG_BIG, F32), jnp.zeros((tq, 1), F32), jnp.zeros((tq, MLA_V), F32))
    carry = lax.fori_loop(0, qi * per_q, lambda j, cr: step(j, cr, None), carry)
    for d in range(per_q):
        carry = step(qi * per_q + d, carry, d)
    _, l, acc = carry
    o_ref[0] = (acc / l).astype(o_ref.dtype)


def _mla_attention(q, k, v):
    bsz, heads, seq, _ = q.shape
    tq = min(seq, 512)
    tk = min(seq, 512)
    return pl.pallas_call(
        functools.partial(_mla_attn_kernel, tq=tq, tk=tk),
        grid=(bsz, heads, seq // tq),
        in_specs=[pl.BlockSpec((1, 1, tq, MLA_QK_PAD), lambda b, h, i: (b, h, i, 0)),
                  pl.BlockSpec((1, 1, seq, MLA_QK_PAD), lambda b, h, i: (b, h, 0, 0)),
                  pl.BlockSpec((1, 1, seq, MLA_V), lambda b, h, i: (b, h, 0, 0))],
        out_specs=pl.BlockSpec((1, tq, MLA_V), lambda b, h, i: (b, i, h)),
        out_shape=jax.ShapeDtypeStruct((bsz, seq, heads * MLA_V), BF16),
        compiler_params=_cparams("parallel", "parallel", "parallel"),
    )(q, k, v)


def _cross_attention(h1, nx_ref, wq_ref, qg_ref, mk_ref, mv_ref, wo_ref):
    hn = _rms(h1, nx_ref[...]).astype(BF16)
    q = _dot(hn, wq_ref[...])
    outs = []
    for hd in range(XA_HEADS):
        sl = slice(hd * XA_DIM, (hd + 1) * XA_DIM)
        qh = (_rms(q[:, sl], qg_ref[...]) * (XA_DIM ** -0.5)).astype(BF16)
        s = _dot_nt(qh, mk_ref[0, :, sl])
        e = jnp.exp(s - jnp.max(s, axis=-1, keepdims=True))
        p = e / jnp.sum(e, axis=-1, keepdims=True)
        outs.append(_dot(p.astype(BF16), mv_ref[0, :, sl]).astype(BF16))
    return h1 + _dot(jnp.concatenate(outs, axis=-1), wo_ref[...])


def _post_proj_kernel(h_ref, ohg_ref, omla_ref, wout_ref, nx_ref, wq_ref, qg_ref, mk_ref, mv_ref,
                      wo_ref, out_ref):
    o = jnp.concatenate([ohg_ref[...], omla_ref[...]], axis=-1)
    h1 = h_ref[...] + _dot(o, wout_ref[...])
    out_ref[...] = _cross_attention(h1, nx_ref, wq_ref, qg_ref, mk_ref, mv_ref, wo_ref)


def _post_glu_kernel(h_ref, y_ref, wa_ref, wb_ref, nx_ref, wq_ref, qg_ref, mk_ref, mv_ref,
                     wo_ref, out_ref):
    y = y_ref[...]
    h1 = h_ref[...] + _dot(y, wa_ref[...]) * _sigmoid(_dot(y, wb_ref[...]))
    out_ref[...] = _cross_attention(h1, nx_ref, wq_ref, qg_ref, mk_ref, mv_ref, wo_ref)


def _post_mixer(kernel_fn, h, acts, mix_weights, norm_xa, wq, q_gain, mem_k, mem_v, wo):
    bsz, seq, d = h.shape
    tokens = bsz * seq
    tm = min(seq, 512)
    per_seq = seq // tm
    mlen = mem_k.shape[1]
    rows = lambda w: pl.BlockSpec((tm, w), lambda i: (i, 0))
    const = lambda shape: pl.BlockSpec(shape, lambda i: (0,) * len(shape))
    mem_spec = pl.BlockSpec((1, mlen, d), lambda i: (i // per_seq, 0, 0))
    out = pl.pallas_call(
        kernel_fn,
        grid=(tokens // tm,),
        in_specs=([rows(d)] + [rows(a.shape[-1]) for a in acts] + [const(w.shape) for w in mix_weights]
                  + [const((1, d)), const((d, d)), const((1, XA_DIM)), mem_spec, mem_spec, const((d, d))]),
        out_specs=rows(d),
        out_shape=jax.ShapeDtypeStruct((tokens, d), F32),
        compiler_params=_cparams("parallel"),
    )(h.reshape(tokens, d), *[a.reshape(tokens, a.shape[-1]) for a in acts], *mix_weights,
      norm_xa.reshape(1, d), wq.astype(BF16), q_gain.reshape(1, XA_DIM), mem_k, mem_v, wo.astype(BF16))
    return out.reshape(bsz, seq, d)


def _ffn_kernel(h_ref, g_ref, wup_ref, cw_ref, cb_ref, wdn_ref, out_ref, carry_ref, *, per_seq, tm):
    @pl.when(pl.program_id(0) % per_seq == 0)
    def _():
        carry_ref[...] = jnp.zeros_like(carry_ref)

    h = h_ref[...]
    hn = _rms(h, g_ref[...]).astype(BF16)

    def chunk(c, acc):
        u = _dot(hn, wup_ref[c])
        xc = jnp.concatenate([carry_ref[c], u], axis=0)
        carry_ref[c] = u[tm - 8:]
        s1 = pltpu.roll(xc, 1, 0)[8:]
        s2 = pltpu.roll(xc, 2, 0)[8:]
        cw = cw_ref[c]
        uc = s2 * cw[0:1] + s1 * cw[1:2] + u * cw[2:3] + cb_ref[c]
        act = _silu(uc[:, :FF_CHUNK]) * uc[:, FF_CHUNK:]
        return acc + _dot(act.astype(BF16), wdn_ref[c])

    out_ref[...] = lax.fori_loop(0, N_FF, chunk, h)


def _conv_ffn(h, gain, w_up, conv_w, conv_b, w_down):
    bsz, seq, d = h.shape
    tokens = bsz * seq
    tm = min(seq, 512)
    per_seq = seq // tm
    fc = FF_CHUNK

    def pair(a):
        lead = a.shape[:-1]
        a = a.reshape(*lead, 2, N_FF, fc)
        a = jnp.moveaxis(a, -2, 0)
        return a.reshape(N_FF, *lead, 2 * fc)

    wup = pair(w_up).astype(BF16)
    cw = pair(conv_w)
    cb = pair(conv_b.reshape(1, -1))
    wdn = w_down.reshape(N_FF, fc, d).astype(BF16)
    const = lambda shape: pl.BlockSpec(shape, lambda i: (0,) * len(shape))
    out = pl.pallas_call(
        functools.partial(_ffn_kernel, per_seq=per_seq, tm=tm),
        grid=(tokens // tm,),
        in_specs=[pl.BlockSpec((tm, d), lambda i: (i, 0)), const((1, d)),
                  const(wup.shape), const(cw.shape), const(cb.shape), const(wdn.shape)],
        out_specs=pl.BlockSpec((tm, d), lambda i: (i, 0)),
        out_shape=jax.ShapeDtypeStruct((tokens, d), F32),
        scratch_shapes=[pltpu.VMEM((N_FF, 8, 2 * fc), F32)],
        compiler_params=_cparams("arbitrary"),
    )(h.reshape(tokens, d), gain.reshape(1, d), wup, cw, cb, wdn)
    return out.reshape(bsz, seq, d)


def _s5_disc_kernel(lr_ref, li_ref, ldt_ref, bre_ref, bim_ref, are_ref, aim_ref, bbr_ref, bbi_ref):
    lr = lr_ref[...]
    li = li_ref[...]
    dt = jnp.exp(ldt_ref[...])
    mag = jnp.exp(lr * dt)
    ab_re = mag * jnp.cos(li * dt)
    ab_im = mag * jnp.sin(li * dt)
    den = lr * lr + li * li
    z_re = ((ab_re - 1.0) * lr + ab_im * li) / den
    z_im = (ab_im * lr - (ab_re - 1.0) * li) / den
    br = bre_ref[...]
    bi = bim_ref[...]
    are_ref[...] = ab_re
    aim_ref[...] = ab_im
    bbr_ref[...] = z_re * br - z_im * bi
    bbi_ref[...] = z_re * bi + z_im * br


def _s5_discretise(lam_re, lam_im, log_dt, b_re, b_im):
    n = S5_GROUPS * S5_STATE
    col = lambda a: a.reshape(n, 1)
    ldt = jnp.repeat(log_dt, S5_STATE).reshape(n, 1)
    vec = jax.ShapeDtypeStruct((n, 1), F32)
    mat = jax.ShapeDtypeStruct((n, S5_GROUP), F32)
    return pl.pallas_call(_s5_disc_kernel, out_shape=[vec, vec, mat, mat])(
        col(lam_re), col(lam_im), ldt, b_re.reshape(n, S5_GROUP), b_im.reshape(n, S5_GROUP))


def _s5_kernel(h_ref, g_ref, wbu_ref, are_ref, aim_ref, wcr_ref, wci_ref, d_ref, y_ref,
               st_ref, hs_ref, *, tm, bsz):
    half = 8 * S5_STATE
    nl = half // 128

    @pl.when(pl.program_id(0) == 0)
    def _():
        st_ref[...] = jnp.zeros_like(st_ref)

    hn = _rms(h_ref[...], g_ref[...])
    u = hn.astype(BF16)
    ys = []
    for j in range(S5_BLOCKS):
        hs_ref[...] = _dot(u[:, j * 128:(j + 1) * 128], wbu_ref[j])
        ar = [jnp.broadcast_to(are_ref[j, :, c * 128:(c + 1) * 128], (bsz, 128)) for c in range(nl)]
        ai = [jnp.broadcast_to(aim_ref[j, :, c * 128:(c + 1) * 128], (bsz, 128)) for c in range(nl)]
        init = tuple(st_ref[:, j * 2 * half + c * 128:j * 2 * half + (c + 1) * 128]
                     for c in range(2 * nl))

        def step(t, carry):
            r0 = pl.multiple_of(t * bsz, bsz)
            new_re, new_im = [], []
            for c in range(nl):
                hr, hi = carry[c], carry[nl + c]
                br = hs_ref[pl.ds(r0, bsz), c * 128:(c + 1) * 128]
                bi = hs_ref[pl.ds(r0, bsz), half + c * 128:half + (c + 1) * 128]
                nr = ar[c] * hr - ai[c] * hi + br
                ni = ar[c] * hi + ai[c] * hr + bi
                hs_ref[pl.ds(r0, bsz), c * 128:(c + 1) * 128] = nr
                hs_ref[pl.ds(r0, bsz), half + c * 128:half + (c + 1) * 128] = ni
                new_re.append(nr)
                new_im.append(ni)
            return tuple(new_re + new_im)

        fin = lax.fori_loop(0, tm // bsz, step, init)
        for c in range(2 * nl):
            st_ref[:, j * 2 * half + c * 128:j * 2 * half + (c + 1) * 128] = fin[c]
        hb = hs_ref[...].astype(BF16)
        ys.append(_dot(hb[:, :half], wcr_ref[j]) - _dot(hb[:, half:], wci_ref[j]))
    y = jnp.concatenate(ys, axis=-1) + d_ref[...] * hn
    y_ref[...] = jax.nn.gelu(y).astype(y_ref.dtype)


def _s5_mixer_core(h, gain, lam_re, lam_im, log_dt, b_re, b_im, c_re, c_im, d_skip):
    bsz, seq, d = h.shape
    assert bsz == 8, "the S5 scan keeps one batch element per sublane"
    ab_re, ab_im, bb_re, bb_im = _s5_discretise(lam_re, lam_im, log_dt, b_re, b_im)
    nb, gl, p, m = S5_BLOCKS, S5_GROUPS // S5_BLOCKS, S5_STATE, S5_GROUP
    eye = jnp.eye(gl, dtype=F32)

    def bu_weight(bb):
        bb = bb.reshape(nb, gl, p, m).transpose(0, 1, 3, 2)
        return (bb[:, :, :, None, :] * eye[None, :, None, :, None]).reshape(nb, gl * m, gl * p)

    def c_weight(cc):
        cc = cc.reshape(nb, gl, m, p).transpose(0, 1, 3, 2)
        return (cc[:, :, :, None, :] * eye[None, :, None, :, None]).reshape(nb, gl * p, gl * m)

    wbu = jnp.concatenate([bu_weight(bb_re), bu_weight(bb_im)], axis=-1).astype(BF16)
    wcr = c_weight(c_re).astype(BF16)
    wci = c_weight(c_im).astype(BF16)
    a_re = ab_re.reshape(nb, 1, gl * p)
    a_im = ab_im.reshape(nb, 1, gl * p)

    rows = seq * bsz
    tm = min(rows, 512)
    x_tm = jnp.transpose(h, (1, 0, 2)).reshape(rows, d)
    const = lambda shape: pl.BlockSpec(shape, lambda i: (0,) * len(shape))
    y = pl.pallas_call(
        functools.partial(_s5_kernel, tm=tm, bsz=bsz),
        grid=(rows // tm,),
        in_specs=[pl.BlockSpec((tm, d), lambda i: (i, 0)), const((1, d)), const(wbu.shape),
                  const(a_re.shape), const(a_im.shape), const(wcr.shape), const(wci.shape),
                  const((1, d))],
        out_specs=pl.BlockSpec((tm, d), lambda i: (i, 0)),
        out_shape=jax.ShapeDtypeStruct((rows, d), BF16),
        scratch_shapes=[pltpu.VMEM((bsz, 2 * S5_GROUPS * S5_STATE), F32),
                        pltpu.VMEM((tm, 2 * gl * p), F32)],
        compiler_params=_cparams("arbitrary"),
    )(x_tm, gain.reshape(1, d), wbu, a_re, a_im, wcr, wci, d_skip.reshape(1, d))
    return jnp.transpose(y.reshape(seq, bsz, d), (1, 0, 2))


def kernel(x, mem, positions, norm_mix, norm_xa, norm_mem, norm_ffn, xa_wq, xa_wk, xa_wv, xa_wo, xa_q_norm, xa_k_norm, ffn_w_up, ffn_conv_w, ffn_conv_b, ffn_w_down, hg_lb_logits, mix_w_in, hg_out_norm, mla_q_a_norm, mla_w_uq, mla_kv_a_norm, mla_w_ukv, mla_qn_nope, mla_qn_rope, mla_kn_nope, mla_kn_rope, mix_w_out, s5_lam_re, s5_lam_im, s5_log_dt, s5_b_re, s5_b_im, s5_c_re, s5_c_im, s5_d, s5_w_glu_a, s5_w_glu_b):
    depth = norm_mix.shape[0]
    ctab, stab = _rope_tables(positions)
    mem_k, mem_v = _memory_kv(mem, norm_mem, xa_wk, xa_wv, xa_k_norm)
    h = x
    for layer in range(depth):
        j = layer // 2
        if layer % 2 == 0:
            hg, q, k, v = _mix_prep(h, norm_mix[layer], mix_w_in[j], mla_q_a_norm[j], mla_w_uq[j],
                                    mla_kv_a_norm[j], mla_w_ukv[j], mla_qn_nope[j], mla_qn_rope[j],
                                    mla_kn_nope[j], mla_kn_rope[j], ctab, stab)
            o_hg = _hgrn(hg, hg_lb_logits, j, hg_out_norm[j])
            o_mla = _mla_attention(q, k, v)
            h = _post_mixer(_post_proj_kernel, h, [o_hg, o_mla], [mix_w_out[j].astype(BF16)],
                            norm_xa[layer], xa_wq[layer], xa_q_norm[layer], mem_k[layer], mem_v[layer],
                            xa_wo[layer])
        else:
            y = _s5_mixer_core(h, norm_mix[layer], s5_lam_re[j], s5_lam_im[j], s5_log_dt[j],
                               s5_b_re[j], s5_b_im[j], s5_c_re[j], s5_c_im[j], s5_d[j])
            h = _post_mixer(_post_glu_kernel, h, [y],
                            [s5_w_glu_a[j].astype(BF16), s5_w_glu_b[j].astype(BF16)],
                            norm_xa[layer], xa_wq[layer], xa_q_norm[layer], mem_k[layer], mem_v[layer],
                            xa_wo[layer])
        h = _conv_ffn(h, norm_ffn[layer], ffn_w_up[layer], ffn_conv_w[layer], ffn_conv_b[layer],
                      ffn_w_down[layer])
    return h
```

```python
import functools

import numpy as np
import jax
import jax.numpy as jnp
from jax import lax
from jax.experimental import pallas as pl
from jax.experimental.pallas import tpu as pltpu

F32 = jnp.float32
BF16 = jnp.bfloat16
EPS = 1e-6
NEG_BIG = -1e30
LOG2_E = 1.4426950408889634

D_MODEL = 1024
HG_HEADS = 4
HG_DIM = 128
HG_WIDTH = HG_HEADS * HG_DIM
HG_CHUNK = 64
HG_LEVELS = (32, 16, 8, 4, 2, 1)
MLA_HEADS = 4
MLA_Q_RANK = 256
MLA_KV_RANK = 128
MLA_NOPE = 128
MLA_ROPE = 64
MLA_V = 128
MLA_QK = MLA_NOPE + MLA_ROPE
MLA_QK_PAD = 256
ATTN_ROW_SPLIT = 1
PREP_ROW_SPLIT = 2
ROPE_BASE = 10000.0
S5_GROUP = 16
S5_GROUPS = D_MODEL // S5_GROUP
S5_STATE = 64
S5_BLOCKS = 8
XA_HEADS = 4
XA_DIM = D_MODEL // XA_HEADS
D_FF = 2816
FF_CHUNK = 256
N_FF = D_FF // FF_CHUNK
IN_PAD = 4 * HG_WIDTH + MLA_Q_RANK + MLA_KV_RANK + 128

VMEM_LIMIT_BYTES = 56 * 1024 * 1024


def _cparams(*sem):
    return pltpu.CompilerParams(dimension_semantics=sem, vmem_limit_bytes=VMEM_LIMIT_BYTES)


def _dot(a, b):
    return jnp.dot(a, b, preferred_element_type=F32)


def _dot_nt(a, b):
    return lax.dot_general(a, b, (((1,), (1,)), ((), ())), preferred_element_type=F32)


def _rms(x, gain, width=None):
    n = x.shape[-1] if width is None else width
    ms = jnp.sum(x * x, axis=-1, keepdims=True) * (1.0 / n)
    return x * lax.rsqrt(ms + EPS) * gain


def _sigmoid(x):
    return 1.0 / (1.0 + jnp.exp(-x))


def _silu(x):
    return x * _sigmoid(x)


def _rope_kernel(pos_ref, invf_ref, cos_ref, sin_ref):
    ang = pos_ref[...].astype(F32) * invf_ref[...]
    cos_ref[...] = jnp.cos(ang)
    sin_ref[...] = jnp.sin(ang)


def _rope_tables(positions):
    bsz, seq = positions.shape
    half = MLA_ROPE // 2
    inv_freq = 1.0 / (ROPE_BASE ** (jnp.arange(0, MLA_ROPE, 2, dtype=F32) / MLA_ROPE))
    rep = 128 // half
    rows = bsz * seq // rep
    pos_rep = jnp.repeat(positions.reshape(rows, rep), half, axis=1)
    invf = jnp.tile(inv_freq, rep).reshape(1, 128)
    tr = min(rows, 1024)
    cos, sin = pl.pallas_call(
        _rope_kernel,
        grid=(rows // tr,),
        in_specs=[pl.BlockSpec((tr, 128), lambda i: (i, 0)),
                  pl.BlockSpec((1, 128), lambda i: (0, 0))],
        out_specs=[pl.BlockSpec((tr, 128), lambda i: (i, 0))] * 2,
        out_shape=[jax.ShapeDtypeStruct((rows, 128), F32)] * 2,
        compiler_params=_cparams("parallel"),
    )(pos_rep, invf)
    cos = cos.reshape(bsz, seq, half)
    sin = sin.reshape(bsz, seq, half)
    zeros = jnp.zeros((bsz, seq, 2 * half), F32)
    ctab = jnp.concatenate([cos, cos, zeros], axis=-1)
    stab = jnp.concatenate([-sin, sin, zeros], axis=-1)
    return ctab, stab


def _memkv_kernel(mem_ref, g_ref, wk_ref, wv_ref, kg_ref, k_ref, v_ref):
    m = _rms(mem_ref[0], g_ref[0]).astype(BF16)
    k = _dot(m, wk_ref[0])
    v = _dot(m, wv_ref[0])
    for h in range(XA_HEADS):
        sl = slice(h * XA_DIM, (h + 1) * XA_DIM)
        k_ref[0, 0, :, sl] = _rms(k[:, sl], kg_ref[0]).astype(BF16)
    v_ref[0, 0] = v.astype(BF16)


def _memory_kv(mem, norm_mem, wk, wv, k_gain):
    depth = norm_mem.shape[0]
    bsz, mlen, d = mem.shape
    out = jax.ShapeDtypeStruct((depth, bsz, mlen, d), BF16)
    return pl.pallas_call(
        _memkv_kernel,
        grid=(depth, bsz),
        in_specs=[pl.BlockSpec((1, mlen, d), lambda l, b: (b, 0, 0)),
                  pl.BlockSpec((1, 1, d), lambda l, b: (l, 0, 0)),
                  pl.BlockSpec((1, d, d), lambda l, b: (l, 0, 0)),
                  pl.BlockSpec((1, d, d), lambda l, b: (l, 0, 0)),
                  pl.BlockSpec((1, 1, XA_DIM), lambda l, b: (l, 0, 0))],
        out_specs=[pl.BlockSpec((1, 1, mlen, d), lambda l, b: (l, b, 0, 0))] * 2,
        out_shape=[out, out],
        compiler_params=_cparams("parallel", "parallel"),
    )(mem, norm_mem.reshape(depth, 1, d), wk.astype(BF16), wv.astype(BF16),
      k_gain.reshape(depth, 1, XA_DIM))


def _mix_prep_kernel(h_ref, gain_ref, win_ref, qa_ref, wuq_ref, kva_ref, wukv_ref,
                     qnn_ref, qnr_ref, knn_ref, knr_ref, ctab_ref, stab_ref,
                     hg_ref, q_ref, k_ref, v_ref, *, tl):
    half = MLA_ROPE // 2
    scale = MLA_QK ** -0.5 * LOG2_E
    ts = tl // PREP_ROW_SPLIT
    for r in range(PREP_ROW_SPLIT):
        rows = slice(r * ts, (r + 1) * ts)
        hn = _rms(h_ref[0, rows, :], gain_ref[...]).astype(BF16)
        proj = _dot(hn, win_ref[...])
        hg_ref[0, rows, :] = proj[:, :4 * HG_WIDTH]
        o = 4 * HG_WIDTH
        c_q = proj[:, o:o + MLA_Q_RANK]
        c_kv = proj[:, o + MLA_Q_RANK:o + MLA_Q_RANK + MLA_KV_RANK]
        k_pe = proj[:, o + MLA_Q_RANK + MLA_KV_RANK:]
        q = _dot(_rms(c_q, qa_ref[...]).astype(BF16), wuq_ref[...])
        kv = _dot(_rms(c_kv, kva_ref[...]).astype(BF16), wukv_ref[...])
        ctab = ctab_ref[0, rows, :]
        stab = stab_ref[0, rows, :]

        def rope(x):
            swapped = pltpu.roll(x, half, 1) + pltpu.roll(x, 128 - half, 1)
            return x * ctab + swapped * stab

        k_rope = rope(_rms(k_pe, knr_ref[...], MLA_ROPE)).astype(BF16)
        for h in range(MLA_HEADS):
            b0 = h * MLA_QK_PAD
            qn = _rms(q[:, b0:b0 + MLA_NOPE], qnn_ref[...])
            qr = rope(_rms(q[:, b0 + MLA_NOPE:b0 + MLA_QK_PAD], qnr_ref[...], MLA_ROPE))
            q_ref[0, h, rows, :MLA_NOPE] = (qn * scale).astype(BF16)
            q_ref[0, h, rows, MLA_NOPE:] = (qr * scale).astype(BF16)
            kn = _rms(kv[:, b0:b0 + MLA_NOPE], knn_ref[...])
            k_ref[0, h, rows, :MLA_NOPE] = kn.astype(BF16)
            k_ref[0, h, rows, MLA_NOPE:] = k_rope
            v_ref[0, h, rows, :] = kv[:, b0 + MLA_NOPE:b0 + MLA_NOPE + MLA_V].astype(BF16)


def _mix_prep(h, gain, w_in, q_a_norm, w_uq, kv_a_norm, w_ukv, qn_nope, qn_rope, kn_nope, kn_rope,
              ctab, stab):
    bsz, seq, d = h.shape
    tl = min(seq, 512)
    w_in_p = jnp.pad(w_in, ((0, 0), (0, IN_PAD - w_in.shape[1]))).astype(BF16)
    w_uq_p = jnp.pad(w_uq.reshape(MLA_Q_RANK, MLA_HEADS, MLA_QK),
                     ((0, 0), (0, 0), (0, MLA_QK_PAD - MLA_QK))).reshape(MLA_Q_RANK, -1).astype(BF16)
    pad64 = lambda g: jnp.pad(g, (0, 128 - MLA_ROPE)).reshape(1, 128)
    row = lambda g: g.reshape(1, -1)
    const = lambda shape: pl.BlockSpec(shape, lambda b, i: (0,) * len(shape))
    hw = MLA_HEADS
    return pl.pallas_call(
        functools.partial(_mix_prep_kernel, tl=tl),
        grid=(bsz, seq // tl),
        in_specs=[pl.BlockSpec((1, tl, d), lambda b, i: (b, i, 0)),
                  const((1, d)), const((d, IN_PAD)),
                  const((1, MLA_Q_RANK)), const((MLA_Q_RANK, hw * MLA_QK_PAD)),
                  const((1, MLA_KV_RANK)), const((MLA_KV_RANK, hw * (MLA_NOPE + MLA_V))),
                  const((1, MLA_NOPE)), const((1, 128)), const((1, MLA_NOPE)), const((1, 128)),
                  pl.BlockSpec((1, tl, 128), lambda b, i: (b, i, 0)),
                  pl.BlockSpec((1, tl, 128), lambda b, i: (b, i, 0))],
        out_specs=[pl.BlockSpec((1, tl, 4 * HG_WIDTH), lambda b, i: (b, i, 0)),
                   pl.BlockSpec((1, hw, tl, MLA_QK_PAD), lambda b, i: (b, 0, i, 0)),
                   pl.BlockSpec((1, hw, tl, MLA_QK_PAD), lambda b, i: (b, 0, i, 0)),
                   pl.BlockSpec((1, hw, tl, MLA_V), lambda b, i: (b, 0, i, 0))],
        out_shape=[jax.ShapeDtypeStruct((bsz, seq, 4 * HG_WIDTH), F32),
                   jax.ShapeDtypeStruct((bsz, hw, seq, MLA_QK_PAD), BF16),
                   jax.ShapeDtypeStruct((bsz, hw, seq, MLA_QK_PAD), BF16),
                   jax.ShapeDtypeStruct((bsz, hw, seq, MLA_V), BF16)],
        compiler_params=_cparams("parallel", "parallel"),
    )(h, row(gain), w_in_p, row(q_a_norm), w_uq_p, row(kv_a_norm), w_ukv.astype(BF16),
      row(qn_nope), pad64(qn_rope), row(kn_nope), pad64(kn_rope), ctab, stab)


def _hgrn_tables():
    c = HG_CHUNK
    blocks = []
    for s in HG_LEVELS:
        lvl = np.zeros((c, c), np.float32)
        for t in range(c):
            mid = (t // (2 * s)) * 2 * s + s
            if t % (2 * s) >= s:
                lvl[t, mid:t + 1] = 1.0
            else:
                lvl[t, t + 1:mid] = 1.0
        blocks.append(lvl)
    blocks.append(np.tril(np.ones((c, c), np.float32)))
    blocks.append(np.triu(np.ones((c, c), np.float32), 1))
    m = np.concatenate(blocks, axis=0)
    return np.concatenate([m, m, m], axis=1)


def _hgrn_kernel(hg_ref, lbl_ref, gn_ref, m3_ref, o_ref, st_ref, *, tb, layer_index):
    c = HG_CHUNK

    @pl.when(pl.program_id(1) == 0)
    def _():
        st_ref[...] = jnp.zeros_like(st_ref)

    lg = lbl_ref[...]
    eg = jnp.exp(lg - jnp.max(lg, axis=0, keepdims=True))
    lb_row = jnp.sum(eg[:layer_index + 1], axis=0, keepdims=True) / jnp.sum(eg, axis=0, keepdims=True)

    ti = lax.broadcasted_iota(jnp.int32, (c, c), 0)
    si = lax.broadcasted_iota(jnp.int32, (c, c), 1)
    xor = ti ^ si
    lower = ti > si
    level_masks = [lower & (xor >= s) & (xor < 2 * s) for s in HG_LEVELS]
    on_diag = ti == si
    m3 = m3_ref[...]

    def chunk(ci, carry):
        r0 = pl.multiple_of(ci * c, c)
        for h in range(HG_HEADS):
            lane = slice(h * HG_DIM, (h + 1) * HG_DIM)
            q = hg_ref[0, pl.ds(r0, c), h * HG_DIM:(h + 1) * HG_DIM]
            fl = hg_ref[0, pl.ds(r0, c), HG_WIDTH + h * HG_DIM:HG_WIDTH + (h + 1) * HG_DIM]
            iv = hg_ref[0, pl.ds(r0, c), 2 * HG_WIDTH + h * HG_DIM:2 * HG_WIDTH + (h + 1) * HG_DIM]
            g = hg_ref[0, pl.ds(r0, c), 3 * HG_WIDTH + h * HG_DIM:3 * HG_WIDTH + (h + 1) * HG_DIM]
            lb = lb_row[:, lane]
            f = lb + (1.0 - lb) * _sigmoid(fl)
            lf = jnp.log(f)
            kk = 1.0 - f
            qf = _silu(q)
            hi = lf.astype(BF16)
            r1 = lf - hi.astype(F32)
            mid = r1.astype(BF16)
            lo = (r1 - mid.astype(F32)).astype(BF16)
            ex = jnp.exp(_dot(m3, jnp.concatenate([hi, mid, lo], axis=0)))
            vb = iv.astype(BF16)
            sc = jnp.where(on_diag, jnp.sum(qf * kk, axis=-1, keepdims=True), 0.0)
            for li in range(len(HG_LEVELS)):
                el = ex[li * c:(li + 1) * c]
                sc = jnp.where(level_masks[li],
                               _dot_nt((qf * el).astype(BF16), (kk * el).astype(BF16)), sc)
            nl = len(HG_LEVELS)
            eb = ex[nl * c:(nl + 1) * c]
            ee = ex[(nl + 1) * c:(nl + 2) * c]
            st = st_ref[h]
            o = _dot(sc.astype(BF16), vb) + _dot_nt((qf * eb).astype(BF16), st.astype(BF16))
            kend = (kk * ee).astype(BF16)
            st_ref[h] = st * eb[c - 1:c, :] + _dot(iv.T.astype(BF16), kend)
            on = _rms(o, gn_ref[:, lane]) * _silu(g)
            o_ref[0, pl.ds(r0, c), h * HG_DIM:(h + 1) * HG_DIM] = on.astype(o_ref.dtype)
        return carry

    lax.fori_loop(0, tb // c, chunk, 0, unroll=4)


def _hgrn(hg, lb_logits, layer_index, out_norm):
    bsz, seq, _ = hg.shape
    tb = min(seq, 512)
    m3 = jnp.asarray(_hgrn_tables(), BF16)
    return pl.pallas_call(
        functools.partial(_hgrn_kernel, tb=tb, layer_index=layer_index),
        grid=(bsz, seq // tb),
        in_specs=[pl.BlockSpec((1, tb, 4 * HG_WIDTH), lambda b, i: (b, i, 0)),
                  pl.BlockSpec(lb_logits.shape, lambda b, i: (0, 0)),
                  pl.BlockSpec((1, HG_WIDTH), lambda b, i: (0, 0)),
                  pl.BlockSpec(m3.shape, lambda b, i: (0, 0))],
        out_specs=pl.BlockSpec((1, tb, HG_WIDTH), lambda b, i: (b, i, 0)),
        out_shape=jax.ShapeDtypeStruct((bsz, seq, HG_WIDTH), BF16),
        scratch_shapes=[pltpu.VMEM((HG_HEADS, HG_DIM, HG_DIM), F32)],
        compiler_params=_cparams("parallel", "arbitrary"),
    )(hg, lb_logits, out_norm.reshape(1, HG_WIDTH), m3)


def _mla_attn_kernel(q_ref, k_ref, v_ref, o_ref, *, tq, tk):
    qi = pl.program_id(2)
    per_q = tq // tk
    th = tq // ATTN_ROW_SPLIT
    row = lax.broadcasted_iota(jnp.int32, (th, tk), 0)
    col = lax.broadcasted_iota(jnp.int32, (th, tk), 1)

    def step(j, carry, diag_index):
        k0 = pl.multiple_of(j * tk, tk)
        kj = k_ref[0, 0, pl.ds(k0, tk), :]
        vj = v_ref[0, 0, pl.ds(k0, tk), :]
        out = []
        for r, (m, l, acc) in enumerate(carry):
            s = _dot_nt(q_ref[0, 0, r * th:(r + 1) * th, :], kj)
            if diag_index is not None:
                s = jnp.where(col + (diag_index * tk - r * th) <= row, s, NEG_BIG)
            m_new = jnp.maximum(m, jnp.max(s, axis=-1, keepdims=True))
            alpha = jnp.exp2(m - m_new)
            p = jnp.exp2(s - m_new)
            l = alpha * l + jnp.sum(p, axis=-1, keepdims=True)
            acc = alpha * acc + _dot(p.astype(BF16), vj)
            out.append((m_new, l, acc))
        return tuple(out)

    carry = tuple((jnp.full((th, 1), NEG_BIG, F32), jnp.zeros((th, 1), F32), jnp.zeros((th, MLA_V), F32))
                  for _ in range(ATTN_ROW_SPLIT))
    carry = lax.fori_loop(0, qi * per_q, lambda j, cr: step(j, cr, None), carry)
    for d in range(per_q):
        carry = step(qi * per_q + d, carry, d)
    for r, (_, l, acc) in enumerate(carry):
        o_ref[0, r * th:(r + 1) * th, :] = (acc / l).astype(o_ref.dtype)


def _mla_attention(q, k, v):
    bsz, heads, seq, _ = q.shape
    tq = min(seq, 512)
    tk = min(seq, 512)
    return pl.pallas_call(
        functools.partial(_mla_attn_kernel, tq=tq, tk=tk),
        grid=(bsz, heads, seq // tq),
        in_specs=[pl.BlockSpec((1, 1, tq, MLA_QK_PAD), lambda b, h, i: (b, h, i, 0)),
                  pl.BlockSpec((1, 1, seq, MLA_QK_PAD), lambda b, h, i: (b, h, 0, 0)),
                  pl.BlockSpec((1, 1, seq, MLA_V), lambda b, h, i: (b, h, 0, 0))],
        out_specs=pl.BlockSpec((1, tq, MLA_V), lambda b, h, i: (b, i, h)),
        out_shape=jax.ShapeDtypeStruct((bsz, seq, heads * MLA_V), BF16),
        compiler_params=_cparams("parallel", "parallel", "parallel"),
    )(q, k, v)


def _cross_attention(h1, nx_ref, wq_ref, qg_ref, mk_ref, mv_ref, wo_ref):
    hn = _rms(h1, nx_ref[...]).astype(BF16)
    q = _dot(hn, wq_ref[...])
    outs = []
    for hd in range(XA_HEADS):
        sl = slice(hd * XA_DIM, (hd + 1) * XA_DIM)
        qh = (_rms(q[:, sl], qg_ref[...]) * (XA_DIM ** -0.5)).astype(BF16)
        s = _dot_nt(qh, mk_ref[0, :, sl])
        e = jnp.exp(s - jnp.max(s, axis=-1, keepdims=True))
        p = e / jnp.sum(e, axis=-1, keepdims=True)
        outs.append(_dot(p.astype(BF16), mv_ref[0, :, sl]).astype(BF16))
    return h1 + _dot(jnp.concatenate(outs, axis=-1), wo_ref[...])


def _post_proj_kernel(h_ref, ohg_ref, omla_ref, wout_ref, nx_ref, wq_ref, qg_ref, mk_ref, mv_ref,
                      wo_ref, out_ref):
    o = jnp.concatenate([ohg_ref[...], omla_ref[...]], axis=-1)
    h1 = h_ref[...] + _dot(o, wout_ref[...])
    out_ref[...] = _cross_attention(h1, nx_ref, wq_ref, qg_ref, mk_ref, mv_ref, wo_ref)


def _post_glu_kernel(h_ref, y_ref, wa_ref, wb_ref, nx_ref, wq_ref, qg_ref, mk_ref, mv_ref,
                     wo_ref, out_ref):
    y = y_ref[...]
    h1 = h_ref[...] + _dot(y, wa_ref[...]) * _sigmoid(_dot(y, wb_ref[...]))
    out_ref[...] = _cross_attention(h1, nx_ref, wq_ref, qg_ref, mk_ref, mv_ref, wo_ref)


def _token_spec(tm, width, per_seq, time_major):
    if time_major:
        return pl.BlockSpec((tm, width), lambda i: (i % per_seq, i // per_seq))
    return pl.BlockSpec((tm, width), lambda i: (i, 0))


def _post_mixer(kernel_fn, bsz, seq, h, acts, mix_weights, norm_xa, wq, q_gain, mem_k, mem_v, wo,
                time_major_in):
    d = D_MODEL
    tokens = bsz * seq
    tm = min(seq, 512)
    per_seq = seq // tm
    mlen = mem_k.shape[1]
    rows = lambda w: _token_spec(tm, w, per_seq, time_major_in)
    const = lambda shape: pl.BlockSpec(shape, lambda i: (0,) * len(shape))
    mem_spec = pl.BlockSpec((1, mlen, d), lambda i: (i // per_seq, 0, 0))
    widths = [a.shape[-1] // bsz if time_major_in else a.shape[-1] for a in acts]
    return pl.pallas_call(
        kernel_fn,
        grid=(tokens // tm,),
        in_specs=([rows(d)] + [rows(w) for w in widths] + [const(w.shape) for w in mix_weights]
                  + [const((1, d)), const((d, d)), const((1, XA_DIM)), mem_spec, mem_spec, const((d, d))]),
        out_specs=_token_spec(tm, d, per_seq, False),
        out_shape=jax.ShapeDtypeStruct((tokens, d), F32),
        compiler_params=_cparams("parallel"),
    )(h, *acts, *mix_weights, norm_xa.reshape(1, d), wq.astype(BF16), q_gain.reshape(1, XA_DIM),
      mem_k, mem_v, wo.astype(BF16))


def _ffn_kernel(h_ref, g_ref, wup_ref, cw_ref, cb_ref, wdn_ref, out_ref, carry_ref, act_ref,
                *, per_seq, tm):
    @pl.when(pl.program_id(0) % per_seq == 0)
    def _():
        carry_ref[...] = jnp.zeros_like(carry_ref)

    h = h_ref[...]
    hn = _rms(h, g_ref[...]).astype(BF16)
    for c in range(N_FF):
        u = _dot(hn, wup_ref[c])
        xc = jnp.concatenate([carry_ref[c], u], axis=0)
        carry_ref[c] = u[tm - 8:]
        s1 = pltpu.roll(xc, 1, 0)[8:]
        s2 = pltpu.roll(xc, 2, 0)[8:]
        cw = cw_ref[c]
        uc = s2 * cw[0:1] + s1 * cw[1:2] + u * cw[2:3] + cb_ref[c]
        act = _silu(uc[:, :FF_CHUNK]) * uc[:, FF_CHUNK:]
        act_ref[:, c * FF_CHUNK:(c + 1) * FF_CHUNK] = act.astype(BF16)
    out_ref[...] = h + _dot(act_ref[...], wdn_ref[...])


def _conv_ffn(bsz, seq, h, gain, w_up, conv_w, conv_b, w_down, out_time_major):
    d = D_MODEL
    tokens = bsz * seq
    tm = min(seq, 512)
    per_seq = seq // tm
    fc = FF_CHUNK

    def pair(a):
        lead = a.shape[:-1]
        a = a.reshape(*lead, 2, N_FF, fc)
        a = jnp.moveaxis(a, -2, 0)
        return a.reshape(N_FF, *lead, 2 * fc)

    wup = pair(w_up).astype(BF16)
    cw = pair(conv_w)
    cb = pair(conv_b.reshape(1, -1))
    wdn = w_down.astype(BF16)
    const = lambda shape: pl.BlockSpec(shape, lambda i: (0,) * len(shape))
    out = pl.pallas_call(
        functools.partial(_ffn_kernel, per_seq=per_seq, tm=tm),
        grid=(tokens // tm,),
        in_specs=[pl.BlockSpec((tm, d), lambda i: (i, 0)), const((1, d)),
                  const(wup.shape), const(cw.shape), const(cb.shape), const(wdn.shape)],
        out_specs=_token_spec(tm, d, per_seq, out_time_major),
        out_shape=jax.ShapeDtypeStruct((seq, bsz * d) if out_time_major else (tokens, d), F32),
        scratch_shapes=[pltpu.VMEM((N_FF, 8, 2 * fc), F32), pltpu.VMEM((tm, D_FF), BF16)],
        compiler_params=_cparams("arbitrary"),
    )(h, gain.reshape(1, d), wup, cw, cb, wdn)
    return out


def _s5_disc_kernel(lr_ref, li_ref, ldt_ref, bre_ref, bim_ref, are_ref, aim_ref, bbr_ref, bbi_ref):
    lr = lr_ref[...]
    li = li_ref[...]
    dt = jnp.exp(ldt_ref[...])
    mag = jnp.exp(lr * dt)
    ab_re = mag * jnp.cos(li * dt)
    ab_im = mag * jnp.sin(li * dt)
    den = lr * lr + li * li
    z_re = ((ab_re - 1.0) * lr + ab_im * li) / den
    z_im = (ab_im * lr - (ab_re - 1.0) * li) / den
    br = bre_ref[...]
    bi = bim_ref[...]
    are_ref[...] = ab_re
    aim_ref[...] = ab_im
    bbr_ref[...] = z_re * br - z_im * bi
    bbi_ref[...] = z_re * bi + z_im * br


def _s5_discretise(lam_re, lam_im, log_dt, b_re, b_im):
    n = S5_GROUPS * S5_STATE
    col = lambda a: a.reshape(n, 1)
    ldt = jnp.repeat(log_dt, S5_STATE).reshape(n, 1)
    vec = jax.ShapeDtypeStruct((n, 1), F32)
    mat = jax.ShapeDtypeStruct((n, S5_GROUP), F32)
    return pl.pallas_call(_s5_disc_kernel, out_shape=[vec, vec, mat, mat])(
        col(lam_re), col(lam_im), ldt, b_re.reshape(n, S5_GROUP), b_im.reshape(n, S5_GROUP))


def _s5_kernel(h_ref, g_ref, wbu_ref, are_ref, aim_ref, wcr_ref, wci_ref, d_ref, y_ref,
               st_ref, hs_ref, *, tm, bsz):
    half = 8 * S5_STATE
    nl = half // 128

    @pl.when(pl.program_id(0) == 0)
    def _():
        st_ref[...] = jnp.zeros_like(st_ref)

    hn = _rms(h_ref[...], g_ref[...])
    u = hn.astype(BF16)
    ys = []
    for j in range(S5_BLOCKS):
        buf = hs_ref.at[j % 2]
        buf[...] = _dot(u[:, j * 128:(j + 1) * 128], wbu_ref[j])
        ar = [jnp.broadcast_to(are_ref[j, :, c * 128:(c + 1) * 128], (bsz, 128)) for c in range(nl)]
        ai = [jnp.broadcast_to(aim_ref[j, :, c * 128:(c + 1) * 128], (bsz, 128)) for c in range(nl)]
        state = [st_ref[:, j * 2 * half + c * 128:j * 2 * half + (c + 1) * 128] for c in range(2 * nl)]
        for t in range(tm // bsz):
            rows = slice(t * bsz, (t + 1) * bsz)
            for c in range(nl):
                re_l = slice(c * 128, (c + 1) * 128)
                im_l = slice(half + c * 128, half + (c + 1) * 128)
                hr, hi = state[c], state[nl + c]
                nr = ar[c] * hr - ai[c] * hi + buf[rows, re_l]
                ni = ar[c] * hi + ai[c] * hr + buf[rows, im_l]
                buf[rows, re_l] = nr
                buf[rows, im_l] = ni
                state[c], state[nl + c] = nr, ni
        for c in range(2 * nl):
            st_ref[:, j * 2 * half + c * 128:j * 2 * half + (c + 1) * 128] = state[c]
        hb = buf[...].astype(BF16)
        ys.append(_dot(hb[:, :half], wcr_ref[j]) - _dot(hb[:, half:], wci_ref[j]))
    y = jnp.concatenate(ys, axis=-1) + d_ref[...] * hn
    y_ref[...] = jax.nn.gelu(y).astype(y_ref.dtype)


def _s5_mixer_core(bsz, seq, h_tm, gain, lam_re, lam_im, log_dt, b_re, b_im, c_re, c_im, d_skip):
    d = D_MODEL
    assert bsz == 8, "the S5 scan keeps one batch element per sublane"
    ab_re, ab_im, bb_re, bb_im = _s5_discretise(lam_re, lam_im, log_dt, b_re, b_im)
    nb, gl, p, m = S5_BLOCKS, S5_GROUPS // S5_BLOCKS, S5_STATE, S5_GROUP
    eye = jnp.eye(gl, dtype=F32)

    def bu_weight(bb):
        bb = bb.reshape(nb, gl, p, m).transpose(0, 1, 3, 2)
        return (bb[:, :, :, None, :] * eye[None, :, None, :, None]).reshape(nb, gl * m, gl * p)

    def c_weight(cc):
        cc = cc.reshape(nb, gl, m, p).transpose(0, 1, 3, 2)
        return (cc[:, :, :, None, :] * eye[None, :, None, :, None]).reshape(nb, gl * p, gl * m)

    wbu = jnp.concatenate([bu_weight(bb_re), bu_weight(bb_im)], axis=-1).astype(BF16)
    wcr = c_weight(c_re).astype(BF16)
    wci = c_weight(c_im).astype(BF16)
    a_re = ab_re.reshape(nb, 1, gl * p)
    a_im = ab_im.reshape(nb, 1, gl * p)

    rows = seq * bsz
    tm = min(rows, 512)
    const = lambda shape: pl.BlockSpec(shape, lambda i: (0,) * len(shape))
    return pl.pallas_call(
        functools.partial(_s5_kernel, tm=tm, bsz=bsz),
        grid=(rows // tm,),
        in_specs=[pl.BlockSpec((tm, d), lambda i: (i, 0)), const((1, d)), const(wbu.shape),
                  const(a_re.shape), const(a_im.shape), const(wcr.shape), const(wci.shape),
                  const((1, d))],
        out_specs=pl.BlockSpec((tm, d), lambda i: (i, 0)),
        out_shape=jax.ShapeDtypeStruct((rows, d), BF16),
        scratch_shapes=[pltpu.VMEM((bsz, 2 * S5_GROUPS * S5_STATE), F32),
                        pltpu.VMEM((2, tm, 2 * gl * p), F32)],
        compiler_params=_cparams("arbitrary"),
    )(h_tm.reshape(rows, d), gain.reshape(1, d), wbu, a_re, a_im, wcr, wci,
      d_skip.reshape(1, d)).reshape(seq, bsz * d)


def kernel(x, mem, positions, norm_mix, norm_xa, norm_mem, norm_ffn, xa_wq, xa_wk, xa_wv, xa_wo, xa_q_norm, xa_k_norm, ffn_w_up, ffn_conv_w, ffn_conv_b, ffn_w_down, hg_lb_logits, mix_w_in, hg_out_norm, mla_q_a_norm, mla_w_uq, mla_kv_a_norm, mla_w_ukv, mla_qn_nope, mla_qn_rope, mla_kn_nope, mla_kn_rope, mix_w_out, s5_lam_re, s5_lam_im, s5_log_dt, s5_b_re, s5_b_im, s5_c_re, s5_c_im, s5_d, s5_w_glu_a, s5_w_glu_b):
    depth = norm_mix.shape[0]
    ctab, stab = _rope_tables(positions)
    mem_k, mem_v = _memory_kv(mem, norm_mem, xa_wk, xa_wv, xa_k_norm)
    bsz, seq, d = x.shape
    tokens = bsz * seq
    h = x.reshape(tokens, d)
    for layer in range(depth):
        j = layer // 2
        if layer % 2 == 0:
            hg, q, k, v = _mix_prep(h.reshape(bsz, seq, d), norm_mix[layer], mix_w_in[j], mla_q_a_norm[j],
                                    mla_w_uq[j], mla_kv_a_norm[j], mla_w_ukv[j], mla_qn_nope[j],
                                    mla_qn_rope[j], mla_kn_nope[j], mla_kn_rope[j], ctab, stab)
            o_hg = _hgrn(hg, hg_lb_logits, j, hg_out_norm[j]).reshape(tokens, HG_WIDTH)
            o_mla = _mla_attention(q, k, v).reshape(tokens, MLA_HEADS * MLA_V)
            h = _post_mixer(_post_proj_kernel, bsz, seq, h, [o_hg, o_mla], [mix_w_out[j].astype(BF16)],
                            norm_xa[layer], xa_wq[layer], xa_q_norm[layer], mem_k[layer], mem_v[layer],
                            xa_wo[layer], time_major_in=False)
        else:
            assert layer > 0, "an S5 layer reads the time-major output of the preceding MLP"
            y = _s5_mixer_core(bsz, seq, h, norm_mix[layer], s5_lam_re[j], s5_lam_im[j], s5_log_dt[j],
                               s5_b_re[j], s5_b_im[j], s5_c_re[j], s5_c_im[j], s5_d[j])
            h = _post_mixer(_post_glu_kernel, bsz, seq, h, [y],
                            [s5_w_glu_a[j].astype(BF16), s5_w_glu_b[j].astype(BF16)],
                            norm_xa[layer], xa_wq[layer], xa_q_norm[layer], mem_k[layer], mem_v[layer],
                            xa_wo[layer], time_major_in=True)
        next_is_s5 = layer + 1 < depth and (layer + 1) % 2 == 1
        h = _conv_ffn(bsz, seq, h, norm_ffn[layer], ffn_w_up[layer], ffn_conv_w[layer],
                      ffn_conv_b[layer], ffn_w_down[layer], out_time_major=next_is_s5)
    return h.reshape(bsz, seq, d)
```

```python
import functools

import numpy as np
import jax
import jax.numpy as jnp
from jax import lax
from jax.experimental import pallas as pl
from jax.experimental.pallas import tpu as pltpu

F32 = jnp.float32
BF16 = jnp.bfloat16
EPS = 1e-6
NEG_BIG = -1e30
LOG2_E = 1.4426950408889634

D_MODEL = 1024
HG_HEADS = 4
HG_DIM = 128
HG_WIDTH = HG_HEADS * HG_DIM
HG_CHUNK = 64
HG_LEVELS = (32, 16, 8, 4, 2, 1)
MLA_HEADS = 4
MLA_Q_RANK = 256
MLA_KV_RANK = 128
MLA_NOPE = 128
MLA_ROPE = 64
MLA_V = 128
MLA_QK = MLA_NOPE + MLA_ROPE
MLA_QK_PAD = 256
ATTN_KEY_SPLIT = 1
ATTN_MAX_SHIFT = 60.0
PREP_ROW_SPLIT = 2
ROPE_BASE = 10000.0
S5_GROUP = 16
S5_GROUPS = D_MODEL // S5_GROUP
S5_STATE = 64
S5_BLOCKS = 8
XA_HEADS = 4
XA_DIM = D_MODEL // XA_HEADS
D_FF = 2816
FF_CHUNK = 256
N_FF = D_FF // FF_CHUNK
IN_PAD = 4 * HG_WIDTH + MLA_Q_RANK + MLA_KV_RANK + 128

VMEM_LIMIT_BYTES = 56 * 1024 * 1024


def _cparams(*sem):
    return pltpu.CompilerParams(dimension_semantics=sem, vmem_limit_bytes=VMEM_LIMIT_BYTES)


def _dot(a, b):
    return jnp.dot(a, b, preferred_element_type=F32)


def _dot_nt(a, b):
    return lax.dot_general(a, b, (((1,), (1,)), ((), ())), preferred_element_type=F32)


def _rms(x, gain, width=None):
    n = x.shape[-1] if width is None else width
    ms = jnp.sum(x * x, axis=-1, keepdims=True) * (1.0 / n)
    return x * lax.rsqrt(ms + EPS) * gain


def _sigmoid(x):
    return 1.0 / (1.0 + jnp.exp(-x))


def _silu(x):
    return x * _sigmoid(x)


def _rope_kernel(pos_ref, invf_ref, cos_ref, sin_ref):
    ang = pos_ref[...].astype(F32) * invf_ref[...]
    cos_ref[...] = jnp.cos(ang)
    sin_ref[...] = jnp.sin(ang)


def _rope_tables(positions):
    bsz, seq = positions.shape
    half = MLA_ROPE // 2
    inv_freq = 1.0 / (ROPE_BASE ** (jnp.arange(0, MLA_ROPE, 2, dtype=F32) / MLA_ROPE))
    rep = 128 // half
    rows = bsz * seq // rep
    pos_rep = jnp.broadcast_to(positions.reshape(rows, rep, 1), (rows, rep, half)).reshape(rows, 128)
    invf = jnp.tile(inv_freq, rep).reshape(1, 128)
    tr = min(rows, 1024)
    cos, sin = pl.pallas_call(
        _rope_kernel,
        grid=(rows // tr,),
        in_specs=[pl.BlockSpec((tr, 128), lambda i: (i, 0)),
                  pl.BlockSpec((1, 128), lambda i: (0, 0))],
        out_specs=[pl.BlockSpec((tr, 128), lambda i: (i, 0))] * 2,
        out_shape=[jax.ShapeDtypeStruct((rows, 128), F32)] * 2,
        compiler_params=_cparams("parallel"),
    )(pos_rep, invf)
    cos = cos.reshape(bsz, seq, half)
    sin = sin.reshape(bsz, seq, half)
    zeros = jnp.zeros((bsz, seq, 2 * half), F32)
    ctab = jnp.concatenate([cos, cos, zeros], axis=-1)
    stab = jnp.concatenate([-sin, sin, zeros], axis=-1)
    return ctab, stab


def _memkv_kernel(mem_ref, g_ref, wk_ref, wv_ref, kg_ref, k_ref, v_ref):
    m = _rms(mem_ref[0], g_ref[0]).astype(BF16)
    k = _dot(m, wk_ref[0])
    v = _dot(m, wv_ref[0])
    for h in range(XA_HEADS):
        sl = slice(h * XA_DIM, (h + 1) * XA_DIM)
        k_ref[0, 0, :, sl] = _rms(k[:, sl], kg_ref[0]).astype(BF16)
    v_ref[0, 0] = v.astype(BF16)


def _memory_kv(mem, norm_mem, wk, wv, k_gain):
    depth = norm_mem.shape[0]
    bsz, mlen, d = mem.shape
    out = jax.ShapeDtypeStruct((depth, bsz, mlen, d), BF16)
    return pl.pallas_call(
        _memkv_kernel,
        grid=(depth, bsz),
        in_specs=[pl.BlockSpec((1, mlen, d), lambda l, b: (b, 0, 0)),
                  pl.BlockSpec((1, 1, d), lambda l, b: (l, 0, 0)),
                  pl.BlockSpec((1, d, d), lambda l, b: (l, 0, 0)),
                  pl.BlockSpec((1, d, d), lambda l, b: (l, 0, 0)),
                  pl.BlockSpec((1, 1, XA_DIM), lambda l, b: (l, 0, 0))],
        out_specs=[pl.BlockSpec((1, 1, mlen, d), lambda l, b: (l, b, 0, 0))] * 2,
        out_shape=[out, out],
        compiler_params=_cparams("parallel", "parallel"),
    )(mem, norm_mem.reshape(depth, 1, d), wk.astype(BF16), wv.astype(BF16),
      k_gain.reshape(depth, 1, XA_DIM))


def _mix_prep_kernel(h_ref, gain_ref, win_ref, qa_ref, wuq_ref, kva_ref, wukv_ref,
                     qnn_ref, qnr_ref, knn_ref, knr_ref, qone_ref, kshift_ref, ctab_ref, stab_ref,
                     hg_ref, q_ref, k_ref, v_ref, *, tl):
    half = MLA_ROPE // 2
    scale = MLA_QK ** -0.5 * LOG2_E
    ts = tl // PREP_ROW_SPLIT
    for r in range(PREP_ROW_SPLIT):
        rows = slice(r * ts, (r + 1) * ts)
        hn = _rms(h_ref[0, rows, :], gain_ref[...]).astype(BF16)
        proj = _dot(hn, win_ref[...])
        hg_ref[0, rows, :] = proj[:, :4 * HG_WIDTH]
        o = 4 * HG_WIDTH
        c_q = proj[:, o:o + MLA_Q_RANK]
        c_kv = proj[:, o + MLA_Q_RANK:o + MLA_Q_RANK + MLA_KV_RANK]
        k_pe = proj[:, o + MLA_Q_RANK + MLA_KV_RANK:]
        q = _dot(_rms(c_q, qa_ref[...]).astype(BF16), wuq_ref[...])
        kv = _dot(_rms(c_kv, kva_ref[...]).astype(BF16), wukv_ref[...])
        ctab = ctab_ref[0, rows, :]
        stab = stab_ref[0, rows, :]

        def rope(x):
            swapped = pltpu.roll(x, half, 1) + pltpu.roll(x, 128 - half, 1)
            return x * ctab + swapped * stab

        k_rope = (rope(_rms(k_pe, knr_ref[...], MLA_ROPE)) + kshift_ref[...]).astype(BF16)
        for h in range(MLA_HEADS):
            b0 = h * MLA_QK_PAD
            qn = _rms(q[:, b0:b0 + MLA_NOPE], qnn_ref[...])
            qr = rope(_rms(q[:, b0 + MLA_NOPE:b0 + MLA_QK_PAD], qnr_ref[...], MLA_ROPE))
            q_ref[0, h, rows, :MLA_NOPE] = (qn * scale).astype(BF16)
            q_ref[0, h, rows, MLA_NOPE:] = (qr * scale + qone_ref[...]).astype(BF16)
            kn = _rms(kv[:, b0:b0 + MLA_NOPE], knn_ref[...])
            k_ref[0, h, rows, :MLA_NOPE] = kn.astype(BF16)
            k_ref[0, h, rows, MLA_NOPE:] = k_rope
            v_ref[0, h, rows, :] = kv[:, b0 + MLA_NOPE:b0 + MLA_NOPE + MLA_V].astype(BF16)


def _mix_prep(h, gain, w_in, q_a_norm, w_uq, kv_a_norm, w_ukv, qn_nope, qn_rope, kn_nope, kn_rope,
              shift, ctab, stab):
    bsz, seq, d = h.shape
    spare = np.zeros((1, 128), np.float32)
    spare[0, MLA_ROPE] = 1.0
    q_one = jnp.asarray(spare)
    k_shift = q_one * (-shift)
    tl = min(seq, 512)
    w_in_p = jnp.pad(w_in, ((0, 0), (0, IN_PAD - w_in.shape[1]))).astype(BF16)
    w_uq_p = jnp.pad(w_uq.reshape(MLA_Q_RANK, MLA_HEADS, MLA_QK),
                     ((0, 0), (0, 0), (0, MLA_QK_PAD - MLA_QK))).reshape(MLA_Q_RANK, -1).astype(BF16)
    pad64 = lambda g: jnp.pad(g, (0, 128 - MLA_ROPE)).reshape(1, 128)
    row = lambda g: g.reshape(1, -1)
    const = lambda shape: pl.BlockSpec(shape, lambda b, i: (0,) * len(shape))
    hw = MLA_HEADS
    return pl.pallas_call(
        functools.partial(_mix_prep_kernel, tl=tl),
        grid=(bsz, seq // tl),
        in_specs=[pl.BlockSpec((1, tl, d), lambda b, i: (b, i, 0)),
                  const((1, d)), const((d, IN_PAD)),
                  const((1, MLA_Q_RANK)), const((MLA_Q_RANK, hw * MLA_QK_PAD)),
                  const((1, MLA_KV_RANK)), const((MLA_KV_RANK, hw * (MLA_NOPE + MLA_V))),
                  const((1, MLA_NOPE)), const((1, 128)), const((1, MLA_NOPE)), const((1, 128)),
                  const((1, 128)), const((1, 128)),
                  pl.BlockSpec((1, tl, 128), lambda b, i: (b, i, 0)),
                  pl.BlockSpec((1, tl, 128), lambda b, i: (b, i, 0))],
        out_specs=[pl.BlockSpec((1, tl, 4 * HG_WIDTH), lambda b, i: (b, i, 0)),
                   pl.BlockSpec((1, hw, tl, MLA_QK_PAD), lambda b, i: (b, 0, i, 0)),
                   pl.BlockSpec((1, hw, tl, MLA_QK_PAD), lambda b, i: (b, 0, i, 0)),
                   pl.BlockSpec((1, hw, tl, MLA_V), lambda b, i: (b, 0, i, 0))],
        out_shape=[jax.ShapeDtypeStruct((bsz, seq, 4 * HG_WIDTH), F32),
                   jax.ShapeDtypeStruct((bsz, hw, seq, MLA_QK_PAD), BF16),
                   jax.ShapeDtypeStruct((bsz, hw, seq, MLA_QK_PAD), BF16),
                   jax.ShapeDtypeStruct((bsz, hw, seq, MLA_V), BF16)],
        compiler_params=_cparams("parallel", "parallel"),
    )(h, row(gain), w_in_p, row(q_a_norm), w_uq_p, row(kv_a_norm), w_ukv.astype(BF16),
      row(qn_nope), pad64(qn_rope), row(kn_nope), pad64(kn_rope), q_one, k_shift, ctab, stab)


def _hgrn_tables():
    c = HG_CHUNK
    blocks = []
    for s in HG_LEVELS:
        lvl = np.zeros((c, c), np.float32)
        for t in range(c):
            mid = (t // (2 * s)) * 2 * s + s
            if t % (2 * s) >= s:
                lvl[t, mid:t + 1] = 1.0
            else:
                lvl[t, t + 1:mid] = 1.0
        blocks.append(lvl)
    blocks.append(np.tril(np.ones((c, c), np.float32)))
    blocks.append(np.triu(np.ones((c, c), np.float32), 1))
    m = np.concatenate(blocks, axis=0)
    return np.concatenate([m, m, m], axis=1)


def _hgrn_kernel(hg_ref, lbl_ref, gn_ref, m3_ref, o_ref, st_ref, *, tb, layer_index):
    c = HG_CHUNK

    @pl.when(pl.program_id(1) == 0)
    def _():
        st_ref[...] = jnp.zeros_like(st_ref)

    lg = lbl_ref[...]
    eg = jnp.exp(lg - jnp.max(lg, axis=0, keepdims=True))
    lb_row = jnp.sum(eg[:layer_index + 1], axis=0, keepdims=True) / jnp.sum(eg, axis=0, keepdims=True)

    ti = lax.broadcasted_iota(jnp.int32, (c, c), 0)
    si = lax.broadcasted_iota(jnp.int32, (c, c), 1)
    xor = ti ^ si
    lower = ti > si
    level_masks = [lower & (xor >= s) & (xor < 2 * s) for s in HG_LEVELS]
    on_diag = ti == si
    m3 = m3_ref[...]

    def chunk(ci, carry):
        r0 = pl.multiple_of(ci * c, c)
        for h in range(HG_HEADS):
            lane = slice(h * HG_DIM, (h + 1) * HG_DIM)
            q = hg_ref[0, pl.ds(r0, c), h * HG_DIM:(h + 1) * HG_DIM]
            fl = hg_ref[0, pl.ds(r0, c), HG_WIDTH + h * HG_DIM:HG_WIDTH + (h + 1) * HG_DIM]
            iv = hg_ref[0, pl.ds(r0, c), 2 * HG_WIDTH + h * HG_DIM:2 * HG_WIDTH + (h + 1) * HG_DIM]
            g = hg_ref[0, pl.ds(r0, c), 3 * HG_WIDTH + h * HG_DIM:3 * HG_WIDTH + (h + 1) * HG_DIM]
            lb = lb_row[:, lane]
            f = lb + (1.0 - lb) * _sigmoid(fl)
            lf = jnp.log(f)
            kk = 1.0 - f
            qf = _silu(q)
            hi = lf.astype(BF16)
            r1 = lf - hi.astype(F32)
            mid = r1.astype(BF16)
            lo = (r1 - mid.astype(F32)).astype(BF16)
            ex = jnp.exp(_dot(m3, jnp.concatenate([hi, mid, lo], axis=0)))
            vb = iv.astype(BF16)
            sc = jnp.where(on_diag, jnp.sum(qf * kk, axis=-1, keepdims=True), 0.0)
            for li in range(len(HG_LEVELS)):
                el = ex[li * c:(li + 1) * c]
                sc = jnp.where(level_masks[li],
                               _dot_nt((qf * el).astype(BF16), (kk * el).astype(BF16)), sc)
            nl = len(HG_LEVELS)
            eb = ex[nl * c:(nl + 1) * c]
            ee = ex[(nl + 1) * c:(nl + 2) * c]
            st = st_ref[h]
            o = _dot(sc.astype(BF16), vb) + _dot_nt((qf * eb).astype(BF16), st.astype(BF16))
            kend = (kk * ee).astype(BF16)
            st_ref[h] = st * eb[c - 1:c, :] + _dot(iv.T.astype(BF16), kend)
            on = _rms(o, gn_ref[:, lane]) * _silu(g)
            o_ref[0, pl.ds(r0, c), h * HG_DIM:(h + 1) * HG_DIM] = on.astype(o_ref.dtype)
        return carry

    lax.fori_loop(0, tb // c, chunk, 0, unroll=4)


def _hgrn(hg, lb_logits, layer_index, out_norm):
    bsz, seq, _ = hg.shape
    tb = min(seq, 512)
    m3 = jnp.asarray(_hgrn_tables(), BF16)
    return pl.pallas_call(
        functools.partial(_hgrn_kernel, tb=tb, layer_index=layer_index),
        grid=(bsz, seq // tb),
        in_specs=[pl.BlockSpec((1, tb, 4 * HG_WIDTH), lambda b, i: (b, i, 0)),
                  pl.BlockSpec(lb_logits.shape, lambda b, i: (0, 0)),
                  pl.BlockSpec((1, HG_WIDTH), lambda b, i: (0, 0)),
                  pl.BlockSpec(m3.shape, lambda b, i: (0, 0))],
        out_specs=pl.BlockSpec((1, tb, HG_WIDTH), lambda b, i: (b, i, 0)),
        out_shape=jax.ShapeDtypeStruct((bsz, seq, HG_WIDTH), BF16),
        scratch_shapes=[pltpu.VMEM((HG_HEADS, HG_DIM, HG_DIM), F32)],
        compiler_params=_cparams("parallel", "arbitrary"),
    )(hg, lb_logits, out_norm.reshape(1, HG_WIDTH), m3)


def _mla_attn_kernel(q_ref, k_ref, v_ref, o_ref, *, tq, tk, bounded):
    qi = pl.program_id(2)
    per_q = tq // tk
    q = q_ref[0, 0]
    row = lax.broadcasted_iota(jnp.int32, (tq, tk), 0)
    col = lax.broadcasted_iota(jnp.int32, (tq, tk), 1)
    row_s = lax.broadcasted_iota(jnp.int32, (tq, tk // ATTN_KEY_SPLIT), 0)
    col_s = lax.broadcasted_iota(jnp.int32, (tq, tk // ATTN_KEY_SPLIT), 1)

    def step(j, carry, diag_index):
        k0 = pl.multiple_of(j * tk, tk)
        if bounded:
            l, acc = carry
            ts = tk // ATTN_KEY_SPLIT
            for u in range(ATTN_KEY_SPLIT):
                s = _dot_nt(q, k_ref[0, 0, pl.ds(k0 + u * ts, ts), :])
                if diag_index is not None:
                    s = jnp.where(col_s + (diag_index * tk + u * ts) <= row_s, s, NEG_BIG)
                p = jnp.exp2(s)
                l = l + jnp.sum(p, axis=-1, keepdims=True)
                acc = acc + _dot(p.astype(BF16), v_ref[0, 0, pl.ds(k0 + u * ts, ts), :])
            return l, acc
        kj = k_ref[0, 0, pl.ds(k0, tk), :]
        vj = v_ref[0, 0, pl.ds(k0, tk), :]
        s = _dot_nt(q, kj)
        if diag_index is not None:
            s = jnp.where(col + diag_index * tk <= row, s, NEG_BIG)
        m, l, acc = carry
        m_new = jnp.maximum(m, jnp.max(s, axis=-1, keepdims=True))
        alpha = jnp.exp2(m - m_new)
        p = jnp.exp2(s - m_new)
        l = alpha * l + jnp.sum(p, axis=-1, keepdims=True)
        acc = alpha * acc + _dot(p.astype(BF16), vj)
        return m_new, l, acc

    carry = (jnp.zeros((tq, 1), F32), jnp.zeros((tq, MLA_V), F32))
    if not bounded:
        carry = (jnp.full((tq, 1), NEG_BIG, F32),) + carry
    carry = lax.fori_loop(0, qi * per_q, lambda j, cr: step(j, cr, None), carry)
    for d in range(per_q):
        carry = step(qi * per_q + d, carry, d)
    o_ref[0] = (carry[-1] / carry[-2]).astype(o_ref.dtype)


def _mla_attention(q, k, v, bounded):
    bsz, heads, seq, _ = q.shape
    tq = min(seq, 512)
    tk = min(seq, 512)
    return pl.pallas_call(
        functools.partial(_mla_attn_kernel, tq=tq, tk=tk, bounded=bounded),
        grid=(bsz, heads, seq // tq),
        in_specs=[pl.BlockSpec((1, 1, tq, MLA_QK_PAD), lambda b, h, i: (b, h, i, 0)),
                  pl.BlockSpec((1, 1, seq, MLA_QK_PAD), lambda b, h, i: (b, h, 0, 0)),
                  pl.BlockSpec((1, 1, seq, MLA_V), lambda b, h, i: (b, h, 0, 0))],
        out_specs=pl.BlockSpec((1, tq, MLA_V), lambda b, h, i: (b, i, h)),
        out_shape=jax.ShapeDtypeStruct((bsz, seq, heads * MLA_V), BF16),
        compiler_params=_cparams("parallel", "parallel", "parallel"),
    )(q, k, v)


def _score_shift(qn_nope, qn_rope, kn_nope, kn_rope):
    def max_norm2(g_nope, g_rope):
        return MLA_NOPE * jnp.max(g_nope * g_nope) + MLA_ROPE * jnp.max(g_rope * g_rope)

    bound = MLA_QK ** -0.5 * LOG2_E * jnp.sqrt(max_norm2(qn_nope, qn_rope) * max_norm2(kn_nope, kn_rope))
    shift = jnp.ceil(1.02 * bound + 1.0)
    use_bounded = shift <= ATTN_MAX_SHIFT
    return use_bounded, jnp.where(use_bounded, shift, 0.0)


def _cross_attention(h1, nx_ref, wq_ref, qg_ref, mk_ref, mv_ref, wo_ref):
    hn = _rms(h1, nx_ref[...]).astype(BF16)
    q = _dot(hn, wq_ref[...])
    outs = []
    for hd in range(XA_HEADS):
        sl = slice(hd * XA_DIM, (hd + 1) * XA_DIM)
        qh = (_rms(q[:, sl], qg_ref[...]) * (XA_DIM ** -0.5)).astype(BF16)
        s = _dot_nt(qh, mk_ref[0, :, sl])
        e = jnp.exp(s - jnp.max(s, axis=-1, keepdims=True))
        p = e / jnp.sum(e, axis=-1, keepdims=True)
        outs.append(_dot(p.astype(BF16), mv_ref[0, :, sl]).astype(BF16))
    return h1 + _dot(jnp.concatenate(outs, axis=-1), wo_ref[...])


def _post_proj_kernel(h_ref, ohg_ref, omla_ref, wout_ref, nx_ref, wq_ref, qg_ref, mk_ref, mv_ref,
                      wo_ref, out_ref):
    o = jnp.concatenate([ohg_ref[...], omla_ref[...]], axis=-1)
    h1 = h_ref[...] + _dot(o, wout_ref[...])
    out_ref[...] = _cross_attention(h1, nx_ref, wq_ref, qg_ref, mk_ref, mv_ref, wo_ref)


def _post_glu_kernel(h_ref, y_ref, wa_ref, wb_ref, nx_ref, wq_ref, qg_ref, mk_ref, mv_ref,
                     wo_ref, out_ref):
    y = y_ref[...]
    h1 = h_ref[...] + _dot(y, wa_ref[...]) * _sigmoid(_dot(y, wb_ref[...]))
    out_ref[...] = _cross_attention(h1, nx_ref, wq_ref, qg_ref, mk_ref, mv_ref, wo_ref)


def _post_mixer(kernel_fn, bsz, seq, h, acts, mix_weights, norm_xa, wq, q_gain, mem_k, mem_v, wo):
    d = D_MODEL
    tokens = bsz * seq
    tm = min(seq, 512)
    per_seq = seq // tm
    mlen = mem_k.shape[1]
    rows = lambda w: pl.BlockSpec((tm, w), lambda i: (i, 0))
    const = lambda shape: pl.BlockSpec(shape, lambda i: (0,) * len(shape))
    mem_spec = pl.BlockSpec((1, mlen, d), lambda i: (i // per_seq, 0, 0))
    return pl.pallas_call(
        kernel_fn,
        grid=(tokens // tm,),
        in_specs=([rows(d)] + [rows(a.shape[-1]) for a in acts] + [const(w.shape) for w in mix_weights]
                  + [const((1, d)), const((d, d)), const((1, XA_DIM)), mem_spec, mem_spec, const((d, d))]),
        out_specs=rows(d),
        out_shape=jax.ShapeDtypeStruct((tokens, d), F32),
        compiler_params=_cparams("parallel"),
    )(h, *acts, *mix_weights, norm_xa.reshape(1, d), wq.astype(BF16), q_gain.reshape(1, XA_DIM),
      mem_k, mem_v, wo.astype(BF16))


def _ffn_kernel(h_ref, g_ref, wup_ref, cw_ref, cb_ref, wdn_ref, out_ref, carry_ref, act_ref,
                *, per_seq, tm):
    @pl.when(pl.program_id(0) % per_seq == 0)
    def _():
        carry_ref[...] = jnp.zeros_like(carry_ref)

    h = h_ref[...]
    hn = _rms(h, g_ref[...]).astype(BF16)

    def up_conv(slot, col0):
        cols = slice(col0, col0 + FF_CHUNK)
        u = _dot(hn, wup_ref[:, cols])
        xc = jnp.concatenate([carry_ref[slot], u], axis=0)
        carry_ref[slot] = u[tm - 8:]
        s1 = pltpu.roll(xc, 1, 0)[8:]
        s2 = pltpu.roll(xc, 2, 0)[8:]
        return s2 * cw_ref[0:1, cols] + s1 * cw_ref[1:2, cols] + u * cw_ref[2:3, cols] + cb_ref[:, cols]

    for c in range(N_FF):
        gate = up_conv(2 * c, c * FF_CHUNK)
        val = up_conv(2 * c + 1, D_FF + c * FF_CHUNK)
        act_ref[:, c * FF_CHUNK:(c + 1) * FF_CHUNK] = (_silu(gate) * val).astype(BF16)
    out_ref[...] = h + _dot(act_ref[...], wdn_ref[...])


def _conv_ffn(bsz, seq, h, gain, w_up, conv_w, conv_b, w_down):
    d = D_MODEL
    tokens = bsz * seq
    tm = min(seq, 512)
    per_seq = seq // tm
    wup = w_up.astype(BF16)
    cw = conv_w
    cb = conv_b.reshape(1, -1)
    wdn = w_down.astype(BF16)
    const = lambda shape: pl.BlockSpec(shape, lambda i: (0,) * len(shape))
    out = pl.pallas_call(
        functools.partial(_ffn_kernel, per_seq=per_seq, tm=tm),
        grid=(tokens // tm,),
        in_specs=[pl.BlockSpec((tm, d), lambda i: (i, 0)), const((1, d)),
                  const(wup.shape), const(cw.shape), const(cb.shape), const(wdn.shape)],
        out_specs=pl.BlockSpec((tm, d), lambda i: (i, 0)),
        out_shape=jax.ShapeDtypeStruct((tokens, d), F32),
        scratch_shapes=[pltpu.VMEM((2 * N_FF, 8, FF_CHUNK), F32), pltpu.VMEM((tm, D_FF), BF16)],
        compiler_params=_cparams("arbitrary"),
    )(h, gain.reshape(1, d), wup, cw, cb, wdn)
    return out


def _s5_disc_kernel(lr_ref, li_ref, ldt_ref, bre_ref, bim_ref, are_ref, aim_ref, bbr_ref, bbi_ref):
    lr = lr_ref[...]
    li = li_ref[...]
    dt = jnp.exp(ldt_ref[...])
    mag = jnp.exp(lr * dt)
    ab_re = mag * jnp.cos(li * dt)
    ab_im = mag * jnp.sin(li * dt)
    den = lr * lr + li * li
    z_re = ((ab_re - 1.0) * lr + ab_im * li) / den
    z_im = (ab_im * lr - (ab_re - 1.0) * li) / den
    br = bre_ref[...]
    bi = bim_ref[...]
    are_ref[...] = ab_re
    aim_ref[...] = ab_im
    bbr_ref[...] = z_re * br - z_im * bi
    bbi_ref[...] = z_re * bi + z_im * br


def _s5_discretise(lam_re, lam_im, log_dt, b_re, b_im):
    n = S5_GROUPS * S5_STATE
    col = lambda a: a.reshape(n, 1)
    ldt = jnp.repeat(log_dt, S5_STATE).reshape(n, 1)
    vec = jax.ShapeDtypeStruct((n, 1), F32)
    mat = jax.ShapeDtypeStruct((n, S5_GROUP), F32)
    return pl.pallas_call(_s5_disc_kernel, out_shape=[vec, vec, mat, mat])(
        col(lam_re), col(lam_im), ldt, b_re.reshape(n, S5_GROUP), b_im.reshape(n, S5_GROUP))


def _s5_kernel(h_ref, g_ref, wbu_ref, are_ref, aim_ref, wcr_ref, wci_ref, d_ref, y_ref,
               st_ref, hs_ref, *, tm, bsz):
    half = 8 * S5_STATE
    nl = half // 128
    nt = tm // bsz

    @pl.when(pl.program_id(0) == 0)
    def _():
        st_ref[...] = jnp.zeros_like(st_ref)

    hn = _rms(pltpu.einshape("btd->tbd", h_ref[...]).reshape(tm, D_MODEL), g_ref[...])
    u = hn.astype(BF16)
    ys = []
    for j in range(S5_BLOCKS):
        buf = hs_ref.at[j % 2]
        buf[...] = _dot(u[:, j * 128:(j + 1) * 128], wbu_ref[j])
        ar = [jnp.broadcast_to(are_ref[j, :, c * 128:(c + 1) * 128], (bsz, 128)) for c in range(nl)]
        ai = [jnp.broadcast_to(aim_ref[j, :, c * 128:(c + 1) * 128], (bsz, 128)) for c in range(nl)]
        state = [st_ref[:, j * 2 * half + c * 128:j * 2 * half + (c + 1) * 128] for c in range(2 * nl)]
        for t in range(nt):
            rows = slice(t * bsz, (t + 1) * bsz)
            for c in range(nl):
                re_l = slice(c * 128, (c + 1) * 128)
                im_l = slice(half + c * 128, half + (c + 1) * 128)
                hr, hi = state[c], state[nl + c]
                nr = ar[c] * hr - ai[c] * hi + buf[rows, re_l]
                ni = ar[c] * hi + ai[c] * hr + buf[rows, im_l]
                buf[rows, re_l] = nr
                buf[rows, im_l] = ni
                state[c], state[nl + c] = nr, ni
        for c in range(2 * nl):
            st_ref[:, j * 2 * half + c * 128:j * 2 * half + (c + 1) * 128] = state[c]
        hb = buf[...].astype(BF16)
        ys.append(_dot(hb[:, :half], wcr_ref[j]) - _dot(hb[:, half:], wci_ref[j]))
    y = jax.nn.gelu(jnp.concatenate(ys, axis=-1) + d_ref[...] * hn)
    y_ref[...] = pltpu.einshape("tbd->btd", y.reshape(nt, bsz, D_MODEL)).astype(y_ref.dtype)


def _s5_mixer_core(bsz, seq, h, gain, lam_re, lam_im, log_dt, b_re, b_im, c_re, c_im, d_skip):
    d = D_MODEL
    assert bsz == 8, "the S5 scan keeps one batch element per sublane"
    ab_re, ab_im, bb_re, bb_im = _s5_discretise(lam_re, lam_im, log_dt, b_re, b_im)
    nb, gl, p, m = S5_BLOCKS, S5_GROUPS // S5_BLOCKS, S5_STATE, S5_GROUP
    eye = jnp.eye(gl, dtype=F32)

    def bu_weight(bb):
        bb = bb.reshape(nb, gl, p, m).transpose(0, 1, 3, 2)
        return (bb[:, :, :, None, :] * eye[None, :, None, :, None]).reshape(nb, gl * m, gl * p)

    def c_weight(cc):
        cc = cc.reshape(nb, gl, m, p).transpose(0, 1, 3, 2)
        return (cc[:, :, :, None, :] * eye[None, :, None, :, None]).reshape(nb, gl * p, gl * m)

    wbu = jnp.concatenate([bu_weight(bb_re), bu_weight(bb_im)], axis=-1).astype(BF16)
    wcr = c_weight(c_re).astype(BF16)
    wci = c_weight(c_im).astype(BF16)
    a_re = ab_re.reshape(nb, 1, gl * p)
    a_im = ab_im.reshape(nb, 1, gl * p)

    nt = min(seq, 64)
    tm = bsz * nt
    const = lambda shape: pl.BlockSpec(shape, lambda i: (0,) * len(shape))
    return pl.pallas_call(
        functools.partial(_s5_kernel, tm=tm, bsz=bsz),
        grid=(seq // nt,),
        in_specs=[pl.BlockSpec((bsz, nt, d), lambda i: (0, i, 0)), const((1, d)), const(wbu.shape),
                  const(a_re.shape), const(a_im.shape), const(wcr.shape), const(wci.shape),
                  const((1, d))],
        out_specs=pl.BlockSpec((bsz, nt, d), lambda i: (0, i, 0)),
        out_shape=jax.ShapeDtypeStruct((bsz, seq, d), BF16),
        scratch_shapes=[pltpu.VMEM((bsz, 2 * S5_GROUPS * S5_STATE), F32),
                        pltpu.VMEM((2, tm, 2 * gl * p), F32)],
        compiler_params=_cparams("arbitrary"),
    )(h.reshape(bsz, seq, d), gain.reshape(1, d), wbu, a_re, a_im, wcr, wci,
      d_skip.reshape(1, d)).reshape(bsz * seq, d)


def kernel(x, mem, positions, norm_mix, norm_xa, norm_mem, norm_ffn, xa_wq, xa_wk, xa_wv, xa_wo, xa_q_norm, xa_k_norm, ffn_w_up, ffn_conv_w, ffn_conv_b, ffn_w_down, hg_lb_logits, mix_w_in, hg_out_norm, mla_q_a_norm, mla_w_uq, mla_kv_a_norm, mla_w_ukv, mla_qn_nope, mla_qn_rope, mla_kn_nope, mla_kn_rope, mix_w_out, s5_lam_re, s5_lam_im, s5_log_dt, s5_b_re, s5_b_im, s5_c_re, s5_c_im, s5_d, s5_w_glu_a, s5_w_glu_b):
    depth = norm_mix.shape[0]
    ctab, stab = _rope_tables(positions)
    mem_k, mem_v = _memory_kv(mem, norm_mem, xa_wk, xa_wv, xa_k_norm)
    bsz, seq, d = x.shape
    tokens = bsz * seq
    h = x.reshape(tokens, d)
    for layer in range(depth):
        j = layer // 2
        if layer % 2 == 0:
            use_bounded, shift = _score_shift(mla_qn_nope[j], mla_qn_rope[j], mla_kn_nope[j], mla_kn_rope[j])
            hg, q, k, v = _mix_prep(h.reshape(bsz, seq, d), norm_mix[layer], mix_w_in[j], mla_q_a_norm[j],
                                    mla_w_uq[j], mla_kv_a_norm[j], mla_w_ukv[j], mla_qn_nope[j],
                                    mla_qn_rope[j], mla_kn_nope[j], mla_kn_rope[j], shift, ctab, stab)
            o_hg = _hgrn(hg, hg_lb_logits, j, hg_out_norm[j]).reshape(tokens, HG_WIDTH)
            o_mla = lax.cond(use_bounded,
                             functools.partial(_mla_attention, bounded=True),
                             functools.partial(_mla_attention, bounded=False),
                             q, k, v).reshape(tokens, MLA_HEADS * MLA_V)
            h = _post_mixer(_post_proj_kernel, bsz, seq, h, [o_hg, o_mla], [mix_w_out[j].astype(BF16)],
                            norm_xa[layer], xa_wq[layer], xa_q_norm[layer], mem_k[layer], mem_v[layer],
                            xa_wo[layer])
        else:
            y = _s5_mixer_core(bsz, seq, h, norm_mix[layer], s5_lam_re[j], s5_lam_im[j], s5_log_dt[j],
                               s5_b_re[j], s5_b_im[j], s5_c_re[j], s5_c_im[j], s5_d[j])
            h = _post_mixer(_post_glu_kernel, bsz, seq, h, [y],
                            [s5_w_glu_a[j].astype(BF16), s5_w_glu_b[j].astype(BF16)],
                            norm_xa[layer], xa_wq[layer], xa_q_norm[layer], mem_k[layer], mem_v[layer],
                            xa_wo[layer])
        h = _conv_ffn(bsz, seq, h, norm_ffn[layer], ffn_w_up[layer], ffn_conv_w[layer],
                      ffn_conv_b[layer], ffn_w_down[layer])
    return h.reshape(bsz, seq, d)
```

```python
import functools

import numpy as np
import jax
import jax.numpy as jnp
from jax import lax
from jax.experimental import pallas as pl
from jax.experimental.pallas import tpu as pltpu

F32 = jnp.float32
BF16 = jnp.bfloat16
EPS = 1e-6
NEG_BIG = -1e30
LOG2_E = 1.4426950408889634

D_MODEL = 1024
HG_HEADS = 4
HG_DIM = 128
HG_WIDTH = HG_HEADS * HG_DIM
HG_CHUNK = 64
HG_LEVELS = (32, 16, 8, 4, 2, 1)
MLA_HEADS = 4
MLA_Q_RANK = 256
MLA_KV_RANK = 128
MLA_NOPE = 128
MLA_ROPE = 64
MLA_V = 128
MLA_QK = MLA_NOPE + MLA_ROPE
MLA_QK_PAD = 256
ATTN_HEADS_PER_STEP = 4
ATTN_MAX_SHIFT = 60.0
FFN_TOKENS = 1024
PREP_ROW_SPLIT = 2
ROPE_BASE = 10000.0
S5_GROUP = 16
S5_GROUPS = D_MODEL // S5_GROUP
S5_STATE = 64
S5_BLOCKS = 8
XA_HEADS = 4
XA_DIM = D_MODEL // XA_HEADS
D_FF = 2816
FF_CHUNK = 256
N_FF = D_FF // FF_CHUNK
IN_PAD = 4 * HG_WIDTH + MLA_Q_RANK + MLA_KV_RANK + 128

VMEM_LIMIT_BYTES = 56 * 1024 * 1024


def _cparams(*sem):
    return pltpu.CompilerParams(dimension_semantics=sem, vmem_limit_bytes=VMEM_LIMIT_BYTES)


def _dot(a, b):
    return jnp.dot(a, b, preferred_element_type=F32)


def _dot_nt(a, b):
    return lax.dot_general(a, b, (((1,), (1,)), ((), ())), preferred_element_type=F32)


def _rms(x, gain, width=None):
    n = x.shape[-1] if width is None else width
    ms = jnp.sum(x * x, axis=-1, keepdims=True) * (1.0 / n)
    return x * lax.rsqrt(ms + EPS) * gain


def _sigmoid(x):
    return 1.0 / (1.0 + jnp.exp(-x))


def _silu(x):
    return x * _sigmoid(x)


def _rope_kernel(pos_ref, invf_ref, cos_ref, sin_ref):
    ang = pos_ref[...].astype(F32) * invf_ref[...]
    cos_ref[...] = jnp.cos(ang)
    sin_ref[...] = jnp.sin(ang)


def _rope_tables(positions):
    bsz, seq = positions.shape
    half = MLA_ROPE // 2
    inv_freq = 1.0 / (ROPE_BASE ** (jnp.arange(0, MLA_ROPE, 2, dtype=F32) / MLA_ROPE))
    rep = 128 // half
    rows = bsz * seq // rep
    pos_rep = jnp.broadcast_to(positions.reshape(rows, rep, 1), (rows, rep, half)).reshape(rows, 128)
    invf = jnp.tile(inv_freq, rep).reshape(1, 128)
    tr = min(rows, 1024)
    cos, sin = pl.pallas_call(
        _rope_kernel,
        grid=(rows // tr,),
        in_specs=[pl.BlockSpec((tr, 128), lambda i: (i, 0)),
                  pl.BlockSpec((1, 128), lambda i: (0, 0))],
        out_specs=[pl.BlockSpec((tr, 128), lambda i: (i, 0))] * 2,
        out_shape=[jax.ShapeDtypeStruct((rows, 128), F32)] * 2,
        compiler_params=_cparams("parallel"),
    )(pos_rep, invf)
    cos = cos.reshape(bsz, seq, half)
    sin = sin.reshape(bsz, seq, half)
    zeros = jnp.zeros((bsz, seq, 2 * half), F32)
    ctab = jnp.concatenate([cos, cos, zeros], axis=-1)
    stab = jnp.concatenate([-sin, sin, zeros], axis=-1)
    return ctab, stab


def _memkv_kernel(mem_ref, g_ref, wk_ref, wv_ref, kg_ref, k_ref, v_ref):
    m = _rms(mem_ref[0], g_ref[0]).astype(BF16)
    k = _dot(m, wk_ref[0])
    v = _dot(m, wv_ref[0])
    for h in range(XA_HEADS):
        sl = slice(h * XA_DIM, (h + 1) * XA_DIM)
        k_ref[0, 0, :, sl] = _rms(k[:, sl], kg_ref[0]).astype(BF16)
    v_ref[0, 0] = v.astype(BF16)


def _memory_kv(mem, norm_mem, wk, wv, k_gain):
    depth = norm_mem.shape[0]
    bsz, mlen, d = mem.shape
    out = jax.ShapeDtypeStruct((depth, bsz, mlen, d), BF16)
    return pl.pallas_call(
        _memkv_kernel,
        grid=(depth, bsz),
        in_specs=[pl.BlockSpec((1, mlen, d), lambda l, b: (b, 0, 0)),
                  pl.BlockSpec((1, 1, d), lambda l, b: (l, 0, 0)),
                  pl.BlockSpec((1, d, d), lambda l, b: (l, 0, 0)),
                  pl.BlockSpec((1, d, d), lambda l, b: (l, 0, 0)),
                  pl.BlockSpec((1, 1, XA_DIM), lambda l, b: (l, 0, 0))],
        out_specs=[pl.BlockSpec((1, 1, mlen, d), lambda l, b: (l, b, 0, 0))] * 2,
        out_shape=[out, out],
        compiler_params=_cparams("parallel", "parallel"),
    )(mem, norm_mem.reshape(depth, 1, d), wk.astype(BF16), wv.astype(BF16),
      k_gain.reshape(depth, 1, XA_DIM))


def _mix_prep_kernel(h_ref, gain_ref, win_ref, qa_ref, wuq_ref, kva_ref, wukv_ref,
                     qnn_ref, qnr_ref, knn_ref, knr_ref, qone_ref, kshift_ref, ctab_ref, stab_ref,
                     hg_ref, q_ref, k_ref, v_ref, *, tl):
    half = MLA_ROPE // 2
    scale = MLA_QK ** -0.5 * LOG2_E
    ts = tl // PREP_ROW_SPLIT
    for r in range(PREP_ROW_SPLIT):
        rows = slice(r * ts, (r + 1) * ts)
        hn = _rms(h_ref[0, rows, :], gain_ref[...]).astype(BF16)
        proj = _dot(hn, win_ref[...])
        hg_ref[0, rows, :] = proj[:, :4 * HG_WIDTH]
        o = 4 * HG_WIDTH
        c_q = proj[:, o:o + MLA_Q_RANK]
        c_kv = proj[:, o + MLA_Q_RANK:o + MLA_Q_RANK + MLA_KV_RANK]
        k_pe = proj[:, o + MLA_Q_RANK + MLA_KV_RANK:]
        q = _dot(_rms(c_q, qa_ref[...]).astype(BF16), wuq_ref[...])
        kv = _dot(_rms(c_kv, kva_ref[...]).astype(BF16), wukv_ref[...])
        ctab = ctab_ref[0, rows, :]
        stab = stab_ref[0, rows, :]

        def rope(x):
            swapped = pltpu.roll(x, half, 1) + pltpu.roll(x, 128 - half, 1)
            return x * ctab + swapped * stab

        k_rope = (rope(_rms(k_pe, knr_ref[...], MLA_ROPE)) + kshift_ref[...]).astype(BF16)
        for h in range(MLA_HEADS):
            b0 = h * MLA_QK_PAD
            qn = _rms(q[:, b0:b0 + MLA_NOPE], qnn_ref[...])
            qr = rope(_rms(q[:, b0 + MLA_NOPE:b0 + MLA_QK_PAD], qnr_ref[...], MLA_ROPE))
            q_ref[0, h, rows, :MLA_NOPE] = (qn * scale).astype(BF16)
            q_ref[0, h, rows, MLA_NOPE:] = (qr * scale + qone_ref[...]).astype(BF16)
            kn = _rms(kv[:, b0:b0 + MLA_NOPE], knn_ref[...])
            k_ref[0, h, rows, :MLA_NOPE] = kn.astype(BF16)
            k_ref[0, h, rows, MLA_NOPE:] = k_rope
            v_ref[0, h, rows, :] = kv[:, b0 + MLA_NOPE:b0 + MLA_NOPE + MLA_V].astype(BF16)


def _mix_prep(h, gain, w_in, q_a_norm, w_uq, kv_a_norm, w_ukv, qn_nope, qn_rope, kn_nope, kn_rope,
              shift, ctab, stab):
    bsz, seq, d = h.shape
    spare = np.zeros((1, 128), np.float32)
    spare[0, MLA_ROPE] = 1.0
    q_one = jnp.asarray(spare)
    k_shift = q_one * (-shift)
    tl = min(seq, 512)
    w_in_p = jnp.pad(w_in, ((0, 0), (0, IN_PAD - w_in.shape[1]))).astype(BF16)
    w_uq_p = jnp.pad(w_uq.reshape(MLA_Q_RANK, MLA_HEADS, MLA_QK),
                     ((0, 0), (0, 0), (0, MLA_QK_PAD - MLA_QK))).reshape(MLA_Q_RANK, -1).astype(BF16)
    pad64 = lambda g: jnp.pad(g, (0, 128 - MLA_ROPE)).reshape(1, 128)
    row = lambda g: g.reshape(1, -1)
    const = lambda shape: pl.BlockSpec(shape, lambda b, i: (0,) * len(shape))
    hw = MLA_HEADS
    return pl.pallas_call(
        functools.partial(_mix_prep_kernel, tl=tl),
        grid=(bsz, seq // tl),
        in_specs=[pl.BlockSpec((1, tl, d), lambda b, i: (b, i, 0)),
                  const((1, d)), const((d, IN_PAD)),
                  const((1, MLA_Q_RANK)), const((MLA_Q_RANK, hw * MLA_QK_PAD)),
                  const((1, MLA_KV_RANK)), const((MLA_KV_RANK, hw * (MLA_NOPE + MLA_V))),
                  const((1, MLA_NOPE)), const((1, 128)), const((1, MLA_NOPE)), const((1, 128)),
                  const((1, 128)), const((1, 128)),
                  pl.BlockSpec((1, tl, 128), lambda b, i: (b, i, 0)),
                  pl.BlockSpec((1, tl, 128), lambda b, i: (b, i, 0))],
        out_specs=[pl.BlockSpec((1, tl, 4 * HG_WIDTH), lambda b, i: (b, i, 0)),
                   pl.BlockSpec((1, hw, tl, MLA_QK_PAD), lambda b, i: (b, 0, i, 0)),
                   pl.BlockSpec((1, hw, tl, MLA_QK_PAD), lambda b, i: (b, 0, i, 0)),
                   pl.BlockSpec((1, hw, tl, MLA_V), lambda b, i: (b, 0, i, 0))],
        out_shape=[jax.ShapeDtypeStruct((bsz, seq, 4 * HG_WIDTH), F32),
                   jax.ShapeDtypeStruct((bsz, hw, seq, MLA_QK_PAD), BF16),
                   jax.ShapeDtypeStruct((bsz, hw, seq, MLA_QK_PAD), BF16),
                   jax.ShapeDtypeStruct((bsz, hw, seq, MLA_V), BF16)],
        compiler_params=_cparams("parallel", "parallel"),
    )(h, row(gain), w_in_p, row(q_a_norm), w_uq_p, row(kv_a_norm), w_ukv.astype(BF16),
      row(qn_nope), pad64(qn_rope), row(kn_nope), pad64(kn_rope), q_one, k_shift, ctab, stab)


def _hgrn_tables():
    c = HG_CHUNK
    blocks = []
    for s in HG_LEVELS:
        lvl = np.zeros((c, c), np.float32)
        for t in range(c):
            mid = (t // (2 * s)) * 2 * s + s
            if t % (2 * s) >= s:
                lvl[t, mid:t + 1] = 1.0
            else:
                lvl[t, t + 1:mid] = 1.0
        blocks.append(lvl)
    blocks.append(np.tril(np.ones((c, c), np.float32)))
    blocks.append(np.triu(np.ones((c, c), np.float32), 1))
    m = np.concatenate(blocks, axis=0)
    return np.concatenate([m, m, m], axis=1)


def _hgrn_kernel(hg_ref, lbl_ref, gn_ref, m3_ref, o_ref, st_ref, *, tb, layer_index):
    c = HG_CHUNK

    @pl.when(pl.program_id(1) == 0)
    def _():
        st_ref[...] = jnp.zeros_like(st_ref)

    lg = lbl_ref[...]
    eg = jnp.exp(lg - jnp.max(lg, axis=0, keepdims=True))
    lb_row = jnp.sum(eg[:layer_index + 1], axis=0, keepdims=True) / jnp.sum(eg, axis=0, keepdims=True)

    ti = lax.broadcasted_iota(jnp.int32, (c, c), 0)
    si = lax.broadcasted_iota(jnp.int32, (c, c), 1)
    xor = ti ^ si
    lower = ti > si
    level_masks = [lower & (xor >= s) & (xor < 2 * s) for s in HG_LEVELS]
    on_diag = ti == si
    m3 = m3_ref[...]

    def chunk(ci, carry):
        r0 = pl.multiple_of(ci * c, c)
        for h in range(HG_HEADS):
            lane = slice(h * HG_DIM, (h + 1) * HG_DIM)
            q = hg_ref[0, pl.ds(r0, c), h * HG_DIM:(h + 1) * HG_DIM]
            fl = hg_ref[0, pl.ds(r0, c), HG_WIDTH + h * HG_DIM:HG_WIDTH + (h + 1) * HG_DIM]
            iv = hg_ref[0, pl.ds(r0, c), 2 * HG_WIDTH + h * HG_DIM:2 * HG_WIDTH + (h + 1) * HG_DIM]
            g = hg_ref[0, pl.ds(r0, c), 3 * HG_WIDTH + h * HG_DIM:3 * HG_WIDTH + (h + 1) * HG_DIM]
            lb = lb_row[:, lane]
            f = lb + (1.0 - lb) * _sigmoid(fl)
            lf = jnp.log(f)
            kk = 1.0 - f
            qf = _silu(q)
            hi = lf.astype(BF16)
            r1 = lf - hi.astype(F32)
            mid = r1.astype(BF16)
            lo = (r1 - mid.astype(F32)).astype(BF16)
            ex = jnp.exp(_dot(m3, jnp.concatenate([hi, mid, lo], axis=0)))
            vb = iv.astype(BF16)
            sc = jnp.where(on_diag, jnp.sum(qf * kk, axis=-1, keepdims=True), 0.0)
            for li in range(len(HG_LEVELS)):
                el = ex[li * c:(li + 1) * c]
                sc = jnp.where(level_masks[li],
                               _dot_nt((qf * el).astype(BF16), (kk * el).astype(BF16)), sc)
            nl = len(HG_LEVELS)
            eb = ex[nl * c:(nl + 1) * c]
            ee = ex[(nl + 1) * c:(nl + 2) * c]
            st = st_ref[h]
            o = _dot(sc.astype(BF16), vb) + _dot_nt((qf * eb).astype(BF16), st.astype(BF16))
            kend = (kk * ee).astype(BF16)
            st_ref[h] = st * eb[c - 1:c, :] + _dot(iv.T.astype(BF16), kend)
            on = _rms(o, gn_ref[:, lane]) * _silu(g)
            o_ref[0, pl.ds(r0, c), h * HG_DIM:(h + 1) * HG_DIM] = on.astype(o_ref.dtype)
        return carry

    lax.fori_loop(0, tb // c, chunk, 0, unroll=True)


def _hgrn(hg, lb_logits, layer_index, out_norm):
    bsz, seq, _ = hg.shape
    tb = min(seq, 512)
    m3 = jnp.asarray(_hgrn_tables(), BF16)
    return pl.pallas_call(
        functools.partial(_hgrn_kernel, tb=tb, layer_index=layer_index),
        grid=(bsz, seq // tb),
        in_specs=[pl.BlockSpec((1, tb, 4 * HG_WIDTH), lambda b, i: (b, i, 0)),
                  pl.BlockSpec(lb_logits.shape, lambda b, i: (0, 0)),
                  pl.BlockSpec((1, HG_WIDTH), lambda b, i: (0, 0)),
                  pl.BlockSpec(m3.shape, lambda b, i: (0, 0))],
        out_specs=pl.BlockSpec((1, tb, HG_WIDTH), lambda b, i: (b, i, 0)),
        out_shape=jax.ShapeDtypeStruct((bsz, seq, HG_WIDTH), BF16),
        scratch_shapes=[pltpu.VMEM((HG_HEADS, HG_DIM, HG_DIM), F32)],
        compiler_params=_cparams("parallel", "arbitrary"),
    )(hg, lb_logits, out_norm.reshape(1, HG_WIDTH), m3)


def _mla_attn_kernel(q_ref, k_ref, v_ref, o_ref, *, tq, tk, bounded):
    assert tq == tk
    qi = pl.program_id(2)
    heads = q_ref.shape[1]
    on_or_below_diag = (lax.broadcasted_iota(jnp.int32, (tq, tk), 1)
                        <= lax.broadcasted_iota(jnp.int32, (tq, tk), 0))

    def step(j, carry, on_diag):
        keys = pl.ds(pl.multiple_of(j * tk, tk), tk)
        out = []
        for h in range(heads):
            s = _dot_nt(q_ref[0, h], k_ref[0, h, keys, :])
            if on_diag:
                s = jnp.where(on_or_below_diag, s, NEG_BIG)
            if bounded:
                l, acc = carry[h]
                p = jnp.exp2(s)
                out.append((l + jnp.sum(p, axis=-1, keepdims=True),
                            acc + _dot(p.astype(BF16), v_ref[0, h, keys, :])))
            else:
                m, l, acc = carry[h]
                m_new = jnp.maximum(m, jnp.max(s, axis=-1, keepdims=True))
                alpha = jnp.exp2(m - m_new)
                p = jnp.exp2(s - m_new)
                out.append((m_new, alpha * l + jnp.sum(p, axis=-1, keepdims=True),
                            alpha * acc + _dot(p.astype(BF16), v_ref[0, h, keys, :])))
        return tuple(out)

    init = (jnp.zeros((tq, 1), F32), jnp.zeros((tq, MLA_V), F32))
    if not bounded:
        init = (jnp.full((tq, 1), NEG_BIG, F32),) + init
    carry = lax.fori_loop(0, qi, lambda j, cr: step(j, cr, False), (init,) * heads)
    carry = step(qi, carry, True)
    for h in range(heads):
        o_ref[0, :, h * MLA_V:(h + 1) * MLA_V] = (carry[h][-1] / carry[h][-2]).astype(o_ref.dtype)


def _mla_attention(q, k, v, bounded):
    bsz, heads, seq, _ = q.shape
    tq = min(seq, 512)
    tk = min(seq, 512)
    hg = ATTN_HEADS_PER_STEP
    return pl.pallas_call(
        functools.partial(_mla_attn_kernel, tq=tq, tk=tk, bounded=bounded),
        grid=(bsz, heads // hg, seq // tq),
        in_specs=[pl.BlockSpec((1, hg, tq, MLA_QK_PAD), lambda b, h, i: (b, h, i, 0)),
                  pl.BlockSpec((1, hg, seq, MLA_QK_PAD), lambda b, h, i: (b, h, 0, 0)),
                  pl.BlockSpec((1, hg, seq, MLA_V), lambda b, h, i: (b, h, 0, 0))],
        out_specs=pl.BlockSpec((1, tq, hg * MLA_V), lambda b, h, i: (b, i, h)),
        out_shape=jax.ShapeDtypeStruct((bsz, seq, heads * MLA_V), BF16),
        compiler_params=_cparams("parallel", "parallel", "parallel"),
    )(q, k, v)


def _score_shift(qn_nope, qn_rope, kn_nope, kn_rope):
    def max_norm2(g_nope, g_rope):
        return MLA_NOPE * jnp.max(g_nope * g_nope) + MLA_ROPE * jnp.max(g_rope * g_rope)

    bound = MLA_QK ** -0.5 * LOG2_E * jnp.sqrt(max_norm2(qn_nope, qn_rope) * max_norm2(kn_nope, kn_rope))
    shift = jnp.ceil(1.02 * bound + 1.0)
    use_bounded = shift <= ATTN_MAX_SHIFT
    return use_bounded, jnp.where(use_bounded, shift, 0.0)


def _cross_attention(h1, nx_ref, wq_ref, qg_ref, mk_ref, mv_ref, wo_ref):
    hn = _rms(h1, nx_ref[...]).astype(BF16)
    q = _dot(hn, wq_ref[...])
    outs = []
    for hd in range(XA_HEADS):
        sl = slice(hd * XA_DIM, (hd + 1) * XA_DIM)
        qh = (_rms(q[:, sl], qg_ref[...]) * (XA_DIM ** -0.5)).astype(BF16)
        s = _dot_nt(qh, mk_ref[0, :, sl])
        e = jnp.exp(s - jnp.max(s, axis=-1, keepdims=True))
        p = e / jnp.sum(e, axis=-1, keepdims=True)
        outs.append(_dot(p.astype(BF16), mv_ref[0, :, sl]).astype(BF16))
    return h1 + _dot(jnp.concatenate(outs, axis=-1), wo_ref[...])


def _post_proj_kernel(h_ref, ohg_ref, omla_ref, wout_ref, nx_ref, wq_ref, qg_ref, mk_ref, mv_ref,
                      wo_ref, out_ref):
    o = jnp.concatenate([ohg_ref[...], omla_ref[...]], axis=-1)
    h1 = h_ref[...] + _dot(o, wout_ref[...])
    out_ref[...] = _cross_attention(h1, nx_ref, wq_ref, qg_ref, mk_ref, mv_ref, wo_ref)


def _post_glu_kernel(h_ref, y_ref, wa_ref, wb_ref, nx_ref, wq_ref, qg_ref, mk_ref, mv_ref,
                     wo_ref, out_ref):
    y = y_ref[...]
    h1 = h_ref[...] + _dot(y, wa_ref[...]) * _sigmoid(_dot(y, wb_ref[...]))
    out_ref[...] = _cross_attention(h1, nx_ref, wq_ref, qg_ref, mk_ref, mv_ref, wo_ref)


def _post_mixer(kernel_fn, bsz, seq, h, acts, mix_weights, norm_xa, wq, q_gain, mem_k, mem_v, wo):
    d = D_MODEL
    tokens = bsz * seq
    tm = min(seq, 512)
    per_seq = seq // tm
    mlen = mem_k.shape[1]
    rows = lambda w: pl.BlockSpec((tm, w), lambda i: (i, 0))
    const = lambda shape: pl.BlockSpec(shape, lambda i: (0,) * len(shape))
    mem_spec = pl.BlockSpec((1, mlen, d), lambda i: (i // per_seq, 0, 0))
    return pl.pallas_call(
        kernel_fn,
        grid=(tokens // tm,),
        in_specs=([rows(d)] + [rows(a.shape[-1]) for a in acts] + [const(w.shape) for w in mix_weights]
                  + [const((1, d)), const((d, d)), const((1, XA_DIM)), mem_spec, mem_spec, const((d, d))]),
        out_specs=rows(d),
        out_shape=jax.ShapeDtypeStruct((tokens, d), F32),
        compiler_params=_cparams("parallel"),
    )(h, *acts, *mix_weights, norm_xa.reshape(1, d), wq.astype(BF16), q_gain.reshape(1, XA_DIM),
      mem_k, mem_v, wo.astype(BF16))


def _ffn_kernel(h_ref, g_ref, wup_ref, cw_ref, cb_ref, wdn_ref, out_ref, carry_ref, act_ref,
                *, per_seq, tm):
    @pl.when(pl.program_id(0) % per_seq == 0)
    def _():
        carry_ref[...] = jnp.zeros_like(carry_ref)

    h = h_ref[...]
    hn = _rms(h, g_ref[...]).astype(BF16)

    def up_conv(slot, col0):
        cols = slice(col0, col0 + FF_CHUNK)
        u = _dot(hn, wup_ref[:, cols])
        xc = jnp.concatenate([carry_ref[slot], u], axis=0)
        carry_ref[slot] = u[tm - 8:]
        s1 = pltpu.roll(xc, 1, 0)[8:]
        s2 = pltpu.roll(xc, 2, 0)[8:]
        return s2 * cw_ref[0:1, cols] + s1 * cw_ref[1:2, cols] + u * cw_ref[2:3, cols] + cb_ref[:, cols]

    for c in range(N_FF):
        gate = up_conv(2 * c, c * FF_CHUNK)
        val = up_conv(2 * c + 1, D_FF + c * FF_CHUNK)
        act_ref[:, c * FF_CHUNK:(c + 1) * FF_CHUNK] = (_silu(gate) * val).astype(BF16)
    out_ref[...] = h + _dot(act_ref[...], wdn_ref[...])


def _conv_ffn(bsz, seq, h, gain, w_up, conv_w, conv_b, w_down):
    d = D_MODEL
    tokens = bsz * seq
    tm = min(seq, FFN_TOKENS)
    per_seq = seq // tm
    wup = w_up.astype(BF16)
    cw = conv_w
    cb = conv_b.reshape(1, -1)
    wdn = w_down.astype(BF16)
    const = lambda shape: pl.BlockSpec(shape, lambda i: (0,) * len(shape), pipeline_mode=pl.Buffered(1))
    out = pl.pallas_call(
        functools.partial(_ffn_kernel, per_seq=per_seq, tm=tm),
        grid=(tokens // tm,),
        in_specs=[pl.BlockSpec((tm, d), lambda i: (i, 0)), const((1, d)),
                  const(wup.shape), const(cw.shape), const(cb.shape), const(wdn.shape)],
        out_specs=pl.BlockSpec((tm, d), lambda i: (i, 0)),
        out_shape=jax.ShapeDtypeStruct((tokens, d), F32),
        scratch_shapes=[pltpu.VMEM((2 * N_FF, 8, FF_CHUNK), F32), pltpu.VMEM((tm, D_FF), BF16)],
        compiler_params=_cparams("arbitrary"),
    )(h, gain.reshape(1, d), wup, cw, cb, wdn)
    return out


def _s5_disc_kernel(lr_ref, li_ref, ldt_ref, bre_ref, bim_ref, are_ref, aim_ref, bbr_ref, bbi_ref):
    lr = lr_ref[...]
    li = li_ref[...]
    dt = jnp.exp(ldt_ref[...])
    mag = jnp.exp(lr * dt)
    ab_re = mag * jnp.cos(li * dt)
    ab_im = mag * jnp.sin(li * dt)
    den = lr * lr + li * li
    z_re = ((ab_re - 1.0) * lr + ab_im * li) / den
    z_im = (ab_im * lr - (ab_re - 1.0) * li) / den
    br = bre_ref[...]
    bi = bim_ref[...]
    are_ref[...] = ab_re
    aim_ref[...] = ab_im
    bbr_ref[...] = z_re * br - z_im * bi
    bbi_ref[...] = z_re * bi + z_im * br


def _s5_discretise(lam_re, lam_im, log_dt, b_re, b_im):
    n = S5_GROUPS * S5_STATE
    col = lambda a: a.reshape(n, 1)
    ldt = jnp.repeat(log_dt, S5_STATE).reshape(n, 1)
    vec = jax.ShapeDtypeStruct((n, 1), F32)
    mat = jax.ShapeDtypeStruct((n, S5_GROUP), F32)
    return pl.pallas_call(_s5_disc_kernel, out_shape=[vec, vec, mat, mat])(
        col(lam_re), col(lam_im), ldt, b_re.reshape(n, S5_GROUP), b_im.reshape(n, S5_GROUP))


def _s5_kernel(h_ref, g_ref, wbu_ref, are_ref, aim_ref, wcr_ref, wci_ref, d_ref, y_ref,
               st_ref, hs_ref, *, tm, bsz):
    half = 8 * S5_STATE
    nl = half // 128
    nt = tm // bsz

    @pl.when(pl.program_id(0) == 0)
    def _():
        st_ref[...] = jnp.zeros_like(st_ref)

    hn = _rms(pltpu.einshape("btd->tbd", h_ref[...]).reshape(tm, D_MODEL), g_ref[...])
    u = hn.astype(BF16)
    ys = []
    for j in range(S5_BLOCKS):
        buf = hs_ref.at[j % 2]
        buf[...] = _dot(u[:, j * 128:(j + 1) * 128], wbu_ref[j])
        ar = [jnp.broadcast_to(are_ref[j, :, c * 128:(c + 1) * 128], (bsz, 128)) for c in range(nl)]
        ai = [jnp.broadcast_to(aim_ref[j, :, c * 128:(c + 1) * 128], (bsz, 128)) for c in range(nl)]
        state = [st_ref[:, j * 2 * half + c * 128:j * 2 * half + (c + 1) * 128] for c in range(2 * nl)]
        for t in range(nt):
            rows = slice(t * bsz, (t + 1) * bsz)
            for c in range(nl):
                re_l = slice(c * 128, (c + 1) * 128)
                im_l = slice(half + c * 128, half + (c + 1) * 128)
                hr, hi = state[c], state[nl + c]
                nr = ar[c] * hr - ai[c] * hi + buf[rows, re_l]
                ni = ar[c] * hi + ai[c] * hr + buf[rows, im_l]
                buf[rows, re_l] = nr
                buf[rows, im_l] = ni
                state[c], state[nl + c] = nr, ni
        for c in range(2 * nl):
            st_ref[:, j * 2 * half + c * 128:j * 2 * half + (c + 1) * 128] = state[c]
        hb = buf[...].astype(BF16)
        ys.append(_dot(hb[:, :half], wcr_ref[j]) - _dot(hb[:, half:], wci_ref[j]))
    y = jax.nn.gelu(jnp.concatenate(ys, axis=-1) + d_ref[...] * hn)
    y_ref[...] = pltpu.einshape("tbd->btd", y.reshape(nt, bsz, D_MODEL)).astype(y_ref.dtype)


def _s5_mixer_core(bsz, seq, h, gain, lam_re, lam_im, log_dt, b_re, b_im, c_re, c_im, d_skip):
    d = D_MODEL
    assert bsz == 8, "the S5 scan keeps one batch element per sublane"
    ab_re, ab_im, bb_re, bb_im = _s5_discretise(lam_re, lam_im, log_dt, b_re, b_im)
    nb, gl, p, m = S5_BLOCKS, S5_GROUPS // S5_BLOCKS, S5_STATE, S5_GROUP
    eye = jnp.eye(gl, dtype=F32)

    def bu_weight(bb):
        bb = bb.reshape(nb, gl, p, m).transpose(0, 1, 3, 2)
        return (bb[:, :, :, None, :] * eye[None, :, None, :, None]).reshape(nb, gl * m, gl * p)

    def c_weight(cc):
        cc = cc.reshape(nb, gl, m, p).transpose(0, 1, 3, 2)
        return (cc[:, :, :, None, :] * eye[None, :, None, :, None]).reshape(nb, gl * p, gl * m)

    wbu = jnp.concatenate([bu_weight(bb_re), bu_weight(bb_im)], axis=-1).astype(BF16)
    wcr = c_weight(c_re).astype(BF16)
    wci = c_weight(c_im).astype(BF16)
    a_re = ab_re.reshape(nb, 1, gl * p)
    a_im = ab_im.reshape(nb, 1, gl * p)

    nt = min(seq, 64)
    tm = bsz * nt
    const = lambda shape: pl.BlockSpec(shape, lambda i: (0,) * len(shape))
    return pl.pallas_call(
        functools.partial(_s5_kernel, tm=tm, bsz=bsz),
        grid=(seq // nt,),
        in_specs=[pl.BlockSpec((bsz, nt, d), lambda i: (0, i, 0)), const((1, d)), const(wbu.shape),
                  const(a_re.shape), const(a_im.shape), const(wcr.shape), const(wci.shape),
                  const((1, d))],
        out_specs=pl.BlockSpec((bsz, nt, d), lambda i: (0, i, 0)),
        out_shape=jax.ShapeDtypeStruct((bsz, seq, d), BF16),
        scratch_shapes=[pltpu.VMEM((bsz, 2 * S5_GROUPS * S5_STATE), F32),
                        pltpu.VMEM((2, tm, 2 * gl * p), F32)],
        compiler_params=_cparams("arbitrary"),
    )(h.reshape(bsz, seq, d), gain.reshape(1, d), wbu, a_re, a_im, wcr, wci,
      d_skip.reshape(1, d)).reshape(bsz * seq, d)


def kernel(x, mem, positions, norm_mix, norm_xa, norm_mem, norm_ffn, xa_wq, xa_wk, xa_wv, xa_wo, xa_q_norm, xa_k_norm, ffn_w_up, ffn_conv_w, ffn_conv_b, ffn_w_down, hg_lb_logits, mix_w_in, hg_out_norm, mla_q_a_norm, mla_w_uq, mla_kv_a_norm, mla_w_ukv, mla_qn_nope, mla_qn_rope, mla_kn_nope, mla_kn_rope, mix_w_out, s5_lam_re, s5_lam_im, s5_log_dt, s5_b_re, s5_b_im, s5_c_re, s5_c_im, s5_d, s5_w_glu_a, s5_w_glu_b):
    depth = norm_mix.shape[0]
    ctab, stab = _rope_tables(positions)
    mem_k, mem_v = _memory_kv(mem, norm_mem, xa_wk, xa_wv, xa_k_norm)
    bsz, seq, d = x.shape
    tokens = bsz * seq
    h = x.reshape(tokens, d)
    for layer in range(depth):
        j = layer // 2
        if layer % 2 == 0:
            use_bounded, shift = _score_shift(mla_qn_nope[j], mla_qn_rope[j], mla_kn_nope[j], mla_kn_rope[j])
            hg, q, k, v = _mix_prep(h.reshape(bsz, seq, d), norm_mix[layer], mix_w_in[j], mla_q_a_norm[j],
                                    mla_w_uq[j], mla_kv_a_norm[j], mla_w_ukv[j], mla_qn_nope[j],
                                    mla_qn_rope[j], mla_kn_nope[j], mla_kn_rope[j], shift, ctab, stab)
            o_hg = _hgrn(hg, hg_lb_logits, j, hg_out_norm[j]).reshape(tokens, HG_WIDTH)
            o_mla = lax.cond(use_bounded,
                             functools.partial(_mla_attention, bounded=True),
                             functools.partial(_mla_attention, bounded=False),
                             q, k, v).reshape(tokens, MLA_HEADS * MLA_V)
            h = _post_mixer(_post_proj_kernel, bsz, seq, h, [o_hg, o_mla], [mix_w_out[j].astype(BF16)],
                            norm_xa[layer], xa_wq[layer], xa_q_norm[layer], mem_k[layer], mem_v[layer],
                            xa_wo[layer])
        else:
            y = _s5_mixer_core(bsz, seq, h, norm_mix[layer], s5_lam_re[j], s5_lam_im[j], s5_log_dt[j],
                               s5_b_re[j], s5_b_im[j], s5_c_re[j], s5_c_im[j], s5_d[j])
            h = _post_mixer(_post_glu_kernel, bsz, seq, h, [y],
                            [s5_w_glu_a[j].astype(BF16), s5_w_glu_b[j].astype(BF16)],
                            norm_xa[layer], xa_wq[layer], xa_q_norm[layer], mem_k[layer], mem_v[layer],
                            xa_wo[layer])
        h = _conv_ffn(bsz, seq, h, norm_ffn[layer], ffn_w_up[layer], ffn_conv_w[layer],
                      ffn_conv_b[layer], ffn_w_down[layer])
    return h.reshape(bsz, seq, d)
```

```python
import functools

import numpy as np
import jax
import jax.numpy as jnp
from jax import lax
from jax.experimental import pallas as pl
from jax.experimental.pallas import tpu as pltpu

F32 = jnp.float32
BF16 = jnp.bfloat16
EPS = 1e-6
NEG_BIG = -1e30
LOG2_E = 1.4426950408889634

D_MODEL = 1024
HG_HEADS = 4
HG_DIM = 128
HG_WIDTH = HG_HEADS * HG_DIM
HG_CHUNK = 64
HG_LEVELS = (32, 16, 8, 4, 2, 1)
MLA_HEADS = 4
MLA_Q_RANK = 256
MLA_KV_RANK = 128
MLA_NOPE = 128
MLA_ROPE = 64
MLA_V = 128
MLA_QK = MLA_NOPE + MLA_ROPE
MLA_QK_PAD = 256
ATTN_HEADS_PER_STEP = 4
ATTN_MAX_SHIFT = 60.0
FFN_TOKENS = 1024
PREP_ROW_SPLIT = 2
ROPE_BASE = 10000.0
S5_GROUP = 16
S5_GROUPS = D_MODEL // S5_GROUP
S5_STATE = 64
S5_BLOCKS = 8
XA_HEADS = 4
XA_DIM = D_MODEL // XA_HEADS
D_FF = 2816
FF_CHUNK = 256
N_FF = D_FF // FF_CHUNK
IN_PAD = 4 * HG_WIDTH + MLA_Q_RANK + MLA_KV_RANK + 128

VMEM_LIMIT_BYTES = 56 * 1024 * 1024


def _cparams(*sem):
    return pltpu.CompilerParams(dimension_semantics=sem, vmem_limit_bytes=VMEM_LIMIT_BYTES)


def _dot(a, b):
    return jnp.dot(a, b, preferred_element_type=F32)


def _dot_nt(a, b):
    return lax.dot_general(a, b, (((1,), (1,)), ((), ())), preferred_element_type=F32)


def _rms(x, gain, width=None):
    n = x.shape[-1] if width is None else width
    ms = jnp.sum(x * x, axis=-1, keepdims=True) * (1.0 / n)
    return x * lax.rsqrt(ms + EPS) * gain


def _sigmoid(x):
    return 0.5 * jnp.tanh(0.5 * x) + 0.5


def _silu(x):
    return x * _sigmoid(x)


def _rope_kernel(pos_ref, invf_ref, cos_ref, sin_ref):
    ang = pos_ref[...].astype(F32) * invf_ref[...]
    cos_ref[...] = jnp.cos(ang)
    sin_ref[...] = jnp.sin(ang)


def _rope_tables(positions):
    bsz, seq = positions.shape
    half = MLA_ROPE // 2
    inv_freq = 1.0 / (ROPE_BASE ** (jnp.arange(0, MLA_ROPE, 2, dtype=F32) / MLA_ROPE))
    rep = 128 // half
    rows = bsz * seq // rep
    pos_rep = jnp.broadcast_to(positions.reshape(rows, rep, 1), (rows, rep, half)).reshape(rows, 128)
    invf = jnp.tile(inv_freq, rep).reshape(1, 128)
    tr = min(rows, 1024)
    cos, sin = pl.pallas_call(
        _rope_kernel,
        grid=(rows // tr,),
        in_specs=[pl.BlockSpec((tr, 128), lambda i: (i, 0)),
                  pl.BlockSpec((1, 128), lambda i: (0, 0))],
        out_specs=[pl.BlockSpec((tr, 128), lambda i: (i, 0))] * 2,
        out_shape=[jax.ShapeDtypeStruct((rows, 128), F32)] * 2,
        compiler_params=_cparams("parallel"),
    )(pos_rep, invf)
    cos = cos.reshape(bsz, seq, half)
    sin = sin.reshape(bsz, seq, half)
    zeros = jnp.zeros((bsz, seq, 2 * half), F32)
    ctab = jnp.concatenate([cos, cos, zeros], axis=-1)
    stab = jnp.concatenate([-sin, sin, zeros], axis=-1)
    return ctab, stab


def _memkv_kernel(mem_ref, g_ref, wk_ref, wv_ref, kg_ref, k_ref, v_ref):
    m = _rms(mem_ref[0], g_ref[0]).astype(BF16)
    k = _dot(m, wk_ref[0])
    v = _dot(m, wv_ref[0])
    for h in range(XA_HEADS):
        sl = slice(h * XA_DIM, (h + 1) * XA_DIM)
        k_ref[0, 0, :, sl] = _rms(k[:, sl], kg_ref[0]).astype(BF16)
    v_ref[0, 0] = v.astype(BF16)


def _memory_kv(mem, norm_mem, wk, wv, k_gain):
    depth = norm_mem.shape[0]
    bsz, mlen, d = mem.shape
    out = jax.ShapeDtypeStruct((depth, bsz, mlen, d), BF16)
    return pl.pallas_call(
        _memkv_kernel,
        grid=(depth, bsz),
        in_specs=[pl.BlockSpec((1, mlen, d), lambda l, b: (b, 0, 0)),
                  pl.BlockSpec((1, 1, d), lambda l, b: (l, 0, 0)),
                  pl.BlockSpec((1, d, d), lambda l, b: (l, 0, 0)),
                  pl.BlockSpec((1, d, d), lambda l, b: (l, 0, 0)),
                  pl.BlockSpec((1, 1, XA_DIM), lambda l, b: (l, 0, 0))],
        out_specs=[pl.BlockSpec((1, 1, mlen, d), lambda l, b: (l, b, 0, 0))] * 2,
        out_shape=[out, out],
        compiler_params=_cparams("parallel", "parallel"),
    )(mem, norm_mem.reshape(depth, 1, d), wk.astype(BF16), wv.astype(BF16),
      k_gain.reshape(depth, 1, XA_DIM))


def _mix_prep_kernel(h_ref, gain_ref, win_ref, qa_ref, wuq_ref, kva_ref, wukv_ref,
                     qnn_ref, qnr_ref, knn_ref, knr_ref, qone_ref, kshift_ref, ctab_ref, stab_ref,
                     hg_ref, q_ref, k_ref, v_ref, *, tl):
    half = MLA_ROPE // 2
    scale = MLA_QK ** -0.5 * LOG2_E
    ts = tl // PREP_ROW_SPLIT
    for r in range(PREP_ROW_SPLIT):
        rows = slice(r * ts, (r + 1) * ts)
        hn = _rms(h_ref[0, rows, :], gain_ref[...]).astype(BF16)
        proj = _dot(hn, win_ref[...])
        hg_ref[0, rows, :] = proj[:, :4 * HG_WIDTH]
        o = 4 * HG_WIDTH
        c_q = proj[:, o:o + MLA_Q_RANK]
        c_kv = proj[:, o + MLA_Q_RANK:o + MLA_Q_RANK + MLA_KV_RANK]
        k_pe = proj[:, o + MLA_Q_RANK + MLA_KV_RANK:]
        q = _dot(_rms(c_q, qa_ref[...]).astype(BF16), wuq_ref[...])
        kv = _dot(_rms(c_kv, kva_ref[...]).astype(BF16), wukv_ref[...])
        ctab = ctab_ref[0, rows, :]
        stab = stab_ref[0, rows, :]

        def rope(x):
            swapped = pltpu.roll(x, half, 1) + pltpu.roll(x, 128 - half, 1)
            return x * ctab + swapped * stab

        k_rope = (rope(_rms(k_pe, knr_ref[...], MLA_ROPE)) + kshift_ref[...]).astype(BF16)
        for h in range(MLA_HEADS):
            b0 = h * MLA_QK_PAD
            qn = _rms(q[:, b0:b0 + MLA_NOPE], qnn_ref[...])
            qr = rope(_rms(q[:, b0 + MLA_NOPE:b0 + MLA_QK_PAD], qnr_ref[...], MLA_ROPE))
            q_ref[0, h, rows, :MLA_NOPE] = (qn * scale).astype(BF16)
            q_ref[0, h, rows, MLA_NOPE:] = (qr * scale + qone_ref[...]).astype(BF16)
            kn = _rms(kv[:, b0:b0 + MLA_NOPE], knn_ref[...])
            k_ref[0, h, rows, :MLA_NOPE] = kn.astype(BF16)
            k_ref[0, h, rows, MLA_NOPE:] = k_rope
            v_ref[0, h, rows, :] = kv[:, b0 + MLA_NOPE:b0 + MLA_NOPE + MLA_V].astype(BF16)


def _mix_prep(h, gain, w_in, q_a_norm, w_uq, kv_a_norm, w_ukv, qn_nope, qn_rope, kn_nope, kn_rope,
              shift, ctab, stab):
    bsz, seq, d = h.shape
    spare = np.zeros((1, 128), np.float32)
    spare[0, MLA_ROPE] = 1.0
    q_one = jnp.asarray(spare)
    k_shift = q_one * (-shift)
    tl = min(seq, 512)
    w_in_p = jnp.pad(w_in, ((0, 0), (0, IN_PAD - w_in.shape[1]))).astype(BF16)
    w_uq_p = jnp.pad(w_uq.reshape(MLA_Q_RANK, MLA_HEADS, MLA_QK),
                     ((0, 0), (0, 0), (0, MLA_QK_PAD - MLA_QK))).reshape(MLA_Q_RANK, -1).astype(BF16)
    pad64 = lambda g: jnp.pad(g, (0, 128 - MLA_ROPE)).reshape(1, 128)
    row = lambda g: g.reshape(1, -1)
    const = lambda shape: pl.BlockSpec(shape, lambda b, i: (0,) * len(shape))
    hw = MLA_HEADS
    return pl.pallas_call(
        functools.partial(_mix_prep_kernel, tl=tl),
        grid=(bsz, seq // tl),
        in_specs=[pl.BlockSpec((1, tl, d), lambda b, i: (b, i, 0)),
                  const((1, d)), const((d, IN_PAD)),
                  const((1, MLA_Q_RANK)), const((MLA_Q_RANK, hw * MLA_QK_PAD)),
                  const((1, MLA_KV_RANK)), const((MLA_KV_RANK, hw * (MLA_NOPE + MLA_V))),
                  const((1, MLA_NOPE)), const((1, 128)), const((1, MLA_NOPE)), const((1, 128)),
                  const((1, 128)), const((1, 128)),
                  pl.BlockSpec((1, tl, 128), lambda b, i: (b, i, 0)),
                  pl.BlockSpec((1, tl, 128), lambda b, i: (b, i, 0))],
        out_specs=[pl.BlockSpec((1, tl, 4 * HG_WIDTH), lambda b, i: (b, i, 0)),
                   pl.BlockSpec((1, hw, tl, MLA_QK_PAD), lambda b, i: (b, 0, i, 0)),
                   pl.BlockSpec((1, hw, tl, MLA_QK_PAD), lambda b, i: (b, 0, i, 0)),
                   pl.BlockSpec((1, hw, tl, MLA_V), lambda b, i: (b, 0, i, 0))],
        out_shape=[jax.ShapeDtypeStruct((bsz, seq, 4 * HG_WIDTH), F32),
                   jax.ShapeDtypeStruct((bsz, hw, seq, MLA_QK_PAD), BF16),
                   jax.ShapeDtypeStruct((bsz, hw, seq, MLA_QK_PAD), BF16),
                   jax.ShapeDtypeStruct((bsz, hw, seq, MLA_V), BF16)],
        compiler_params=_cparams("parallel", "parallel"),
    )(h, row(gain), w_in_p, row(q_a_norm), w_uq_p, row(kv_a_norm), w_ukv.astype(BF16),
      row(qn_nope), pad64(qn_rope), row(kn_nope), pad64(kn_rope), q_one, k_shift, ctab, stab)


def _hgrn_tables():
    m = np.tril(np.ones((HG_CHUNK, HG_CHUNK), np.float32))
    return np.concatenate([m, m, m], axis=1)


def _hgrn_level_reference(b, s, row_in_tile):
    c, n = b.shape
    if s >= 4:
        return jnp.concatenate([jnp.broadcast_to(b[r0 + s - 1:r0 + s], (2 * s, n))
                                for r0 in range(0, c, 2 * s)], axis=0)
    if s == 2:
        first = jnp.concatenate([jnp.broadcast_to(b[r0 + 1:r0 + 2], (8, n)) for r0 in range(0, c, 8)], axis=0)
        second = jnp.concatenate([jnp.broadcast_to(b[r0 + 5:r0 + 6], (8, n)) for r0 in range(0, c, 8)], axis=0)
        return jnp.where(row_in_tile < 4, first, second)
    assert s == 1
    return jnp.where(row_in_tile % 2 == 1, pltpu.roll(b, 1, 0), b)


def _hgrn_kernel(hg_ref, lbl_ref, gn_ref, m3_ref, o_ref, st_ref, *, tb, layer_index):
    c = HG_CHUNK

    @pl.when(pl.program_id(1) == 0)
    def _():
        st_ref[...] = jnp.zeros_like(st_ref)

    lg = lbl_ref[...]
    eg = jnp.exp(lg - jnp.max(lg, axis=0, keepdims=True))
    lb_row = jnp.sum(eg[:layer_index + 1], axis=0, keepdims=True) / jnp.sum(eg, axis=0, keepdims=True)

    ti = lax.broadcasted_iota(jnp.int32, (c, c), 0)
    si = lax.broadcasted_iota(jnp.int32, (c, c), 1)
    xor = ti ^ si
    lower = ti > si
    level_masks = [lower & (xor >= s) & (xor < 2 * s) for s in HG_LEVELS]
    on_diag = ti == si
    row = lax.broadcasted_iota(jnp.int32, (c, HG_DIM), 0)
    row_in_tile = row % 8
    level_signs = [jnp.where((row & s) != 0, LOG2_E, -LOG2_E) for s in HG_LEVELS]
    m3 = m3_ref[...]

    states = [st_ref[h] for h in range(HG_HEADS)]
    for ci in range(tb // c):
        rows = slice(ci * c, (ci + 1) * c)
        for h in range(HG_HEADS):
            lane = slice(h * HG_DIM, (h + 1) * HG_DIM)
            q = hg_ref[0, rows, h * HG_DIM:(h + 1) * HG_DIM]
            fl = hg_ref[0, rows, HG_WIDTH + h * HG_DIM:HG_WIDTH + (h + 1) * HG_DIM]
            iv = hg_ref[0, rows, 2 * HG_WIDTH + h * HG_DIM:2 * HG_WIDTH + (h + 1) * HG_DIM]
            g = hg_ref[0, rows, 3 * HG_WIDTH + h * HG_DIM:3 * HG_WIDTH + (h + 1) * HG_DIM]
            lb = lb_row[:, lane]
            f = lb + (1.0 - lb) * _sigmoid(fl)
            lf = jnp.log(f)
            kk = 1.0 - f
            qf = _silu(q)
            hi = lf.astype(BF16)
            r1 = lf - hi.astype(F32)
            mid = r1.astype(BF16)
            lo = (r1 - mid.astype(F32)).astype(BF16)
            b = _dot(m3, jnp.concatenate([hi, mid, lo], axis=0))
            vb = iv.astype(BF16)
            sc = jnp.where(on_diag, jnp.sum(qf * kk, axis=-1, keepdims=True), 0.0)
            for li, s in enumerate(HG_LEVELS):
                el = jnp.exp2((b - _hgrn_level_reference(b, s, row_in_tile)) * level_signs[li])
                sc = jnp.where(level_masks[li],
                               _dot_nt((qf * el).astype(BF16), (kk * el).astype(BF16)), sc)
            eb = jnp.exp(b)
            ee = jnp.exp(b[c - 1:c, :] - b)
            st = states[h]
            o = _dot(sc.astype(BF16), vb) + _dot_nt((qf * eb).astype(BF16), st.astype(BF16))
            kend = (kk * ee).astype(BF16)
            states[h] = st * eb[c - 1:c, :] + _dot(iv.T.astype(BF16), kend)
            on = _rms(o, gn_ref[:, lane]) * _silu(g)
            o_ref[0, rows, h * HG_DIM:(h + 1) * HG_DIM] = on.astype(o_ref.dtype)
    for h in range(HG_HEADS):
        st_ref[h] = states[h]


def _hgrn(hg, lb_logits, layer_index, out_norm):
    bsz, seq, _ = hg.shape
    tb = min(seq, 512)
    m3 = jnp.asarray(_hgrn_tables(), BF16)
    return pl.pallas_call(
        functools.partial(_hgrn_kernel, tb=tb, layer_index=layer_index),
        grid=(bsz, seq // tb),
        in_specs=[pl.BlockSpec((1, tb, 4 * HG_WIDTH), lambda b, i: (b, i, 0)),
                  pl.BlockSpec(lb_logits.shape, lambda b, i: (0, 0)),
                  pl.BlockSpec((1, HG_WIDTH), lambda b, i: (0, 0)),
                  pl.BlockSpec(m3.shape, lambda b, i: (0, 0))],
        out_specs=pl.BlockSpec((1, tb, HG_WIDTH), lambda b, i: (b, i, 0)),
        out_shape=jax.ShapeDtypeStruct((bsz, seq, HG_WIDTH), BF16),
        scratch_shapes=[pltpu.VMEM((HG_HEADS, HG_DIM, HG_DIM), F32)],
        compiler_params=_cparams("parallel", "arbitrary"),
    )(hg, lb_logits, out_norm.reshape(1, HG_WIDTH), m3)


def _mla_attn_kernel(q_ref, k_ref, v_ref, o_ref, *, tq, tk, bounded):
    assert tq == tk
    qi = pl.program_id(2)
    heads = q_ref.shape[1]
    on_or_below_diag = (lax.broadcasted_iota(jnp.int32, (tq, tk), 1)
                        <= lax.broadcasted_iota(jnp.int32, (tq, tk), 0))

    def step(j, carry, on_diag):
        keys = pl.ds(pl.multiple_of(j * tk, tk), tk)
        out = []
        for h in range(heads):
            s = _dot_nt(q_ref[0, h], k_ref[0, h, keys, :])
            if on_diag:
                s = jnp.where(on_or_below_diag, s, NEG_BIG)
            if bounded:
                l, acc = carry[h]
                p = jnp.exp2(s)
                out.append((l + jnp.sum(p, axis=-1, keepdims=True),
                            acc + _dot(p.astype(BF16), v_ref[0, h, keys, :])))
            else:
                m, l, acc = carry[h]
                m_new = jnp.maximum(m, jnp.max(s, axis=-1, keepdims=True))
                alpha = jnp.exp2(m - m_new)
                p = jnp.exp2(s - m_new)
                out.append((m_new, alpha * l + jnp.sum(p, axis=-1, keepdims=True),
                            alpha * acc + _dot(p.astype(BF16), v_ref[0, h, keys, :])))
        return tuple(out)

    init = (jnp.zeros((tq, 1), F32), jnp.zeros((tq, MLA_V), F32))
    if not bounded:
        init = (jnp.full((tq, 1), NEG_BIG, F32),) + init
    carry = lax.fori_loop(0, qi, lambda j, cr: step(j, cr, False), (init,) * heads)
    carry = step(qi, carry, True)
    for h in range(heads):
        o_ref[0, :, h * MLA_V:(h + 1) * MLA_V] = (carry[h][-1] / carry[h][-2]).astype(o_ref.dtype)


def _mla_attention(q, k, v, bounded):
    bsz, heads, seq, _ = q.shape
    tq = min(seq, 512)
    tk = min(seq, 512)
    hg = ATTN_HEADS_PER_STEP
    return pl.pallas_call(
        functools.partial(_mla_attn_kernel, tq=tq, tk=tk, bounded=bounded),
        grid=(bsz, heads // hg, seq // tq),
        in_specs=[pl.BlockSpec((1, hg, tq, MLA_QK_PAD), lambda b, h, i: (b, h, i, 0)),
                  pl.BlockSpec((1, hg, seq, MLA_QK_PAD), lambda b, h, i: (b, h, 0, 0)),
                  pl.BlockSpec((1, hg, seq, MLA_V), lambda b, h, i: (b, h, 0, 0))],
        out_specs=pl.BlockSpec((1, tq, hg * MLA_V), lambda b, h, i: (b, i, h)),
        out_shape=jax.ShapeDtypeStruct((bsz, seq, heads * MLA_V), BF16),
        compiler_params=_cparams("parallel", "parallel", "parallel"),
    )(q, k, v)


def _score_shift(qn_nope, qn_rope, kn_nope, kn_rope):
    def max_norm2(g_nope, g_rope):
        return MLA_NOPE * jnp.max(g_nope * g_nope) + MLA_ROPE * jnp.max(g_rope * g_rope)

    bound = MLA_QK ** -0.5 * LOG2_E * jnp.sqrt(max_norm2(qn_nope, qn_rope) * max_norm2(kn_nope, kn_rope))
    shift = jnp.ceil(1.02 * bound + 1.0)
    use_bounded = shift <= ATTN_MAX_SHIFT
    return use_bounded, jnp.where(use_bounded, shift, 0.0)


def _cross_attention(h1, nx_ref, wq_ref, qg_ref, mk_ref, mv_ref, wo_ref):
    hn = _rms(h1, nx_ref[...]).astype(BF16)
    q = _dot(hn, wq_ref[...])
    outs = []
    for hd in range(XA_HEADS):
        sl = slice(hd * XA_DIM, (hd + 1) * XA_DIM)
        qh = (_rms(q[:, sl], qg_ref[...]) * (XA_DIM ** -0.5)).astype(BF16)
        s = _dot_nt(qh, mk_ref[0, :, sl])
        e = jnp.exp(s - jnp.max(s, axis=-1, keepdims=True))
        p = e / jnp.sum(e, axis=-1, keepdims=True)
        outs.append(_dot(p.astype(BF16), mv_ref[0, :, sl]).astype(BF16))
    return h1 + _dot(jnp.concatenate(outs, axis=-1), wo_ref[...])


def _post_proj_kernel(h_ref, ohg_ref, omla_ref, wout_ref, nx_ref, wq_ref, qg_ref, mk_ref, mv_ref,
                      wo_ref, out_ref):
    o = jnp.concatenate([ohg_ref[...], omla_ref[...]], axis=-1)
    h1 = h_ref[...] + _dot(o, wout_ref[...])
    out_ref[...] = _cross_attention(h1, nx_ref, wq_ref, qg_ref, mk_ref, mv_ref, wo_ref)


def _post_glu_kernel(h_ref, y_ref, wa_ref, wb_ref, nx_ref, wq_ref, qg_ref, mk_ref, mv_ref,
                     wo_ref, out_ref):
    y = y_ref[...]
    h1 = h_ref[...] + _dot(y, wa_ref[...]) * _sigmoid(_dot(y, wb_ref[...]))
    out_ref[...] = _cross_attention(h1, nx_ref, wq_ref, qg_ref, mk_ref, mv_ref, wo_ref)


def _post_mixer(kernel_fn, bsz, seq, h, acts, mix_weights, norm_xa, wq, q_gain, mem_k, mem_v, wo):
    d = D_MODEL
    tokens = bsz * seq
    tm = min(seq, 512)
    per_seq = seq // tm
    mlen = mem_k.shape[1]
    rows = lambda w: pl.BlockSpec((tm, w), lambda i: (i, 0))
    const = lambda shape: pl.BlockSpec(shape, lambda i: (0,) * len(shape))
    mem_spec = pl.BlockSpec((1, mlen, d), lambda i: (i // per_seq, 0, 0))
    return pl.pallas_call(
        kernel_fn,
        grid=(tokens // tm,),
        in_specs=([rows(d)] + [rows(a.shape[-1]) for a in acts] + [const(w.shape) for w in mix_weights]
                  + [const((1, d)), const((d, d)), const((1, XA_DIM)), mem_spec, mem_spec, const((d, d))]),
        out_specs=rows(d),
        out_shape=jax.ShapeDtypeStruct((tokens, d), F32),
        compiler_params=_cparams("parallel"),
    )(h, *acts, *mix_weights, norm_xa.reshape(1, d), wq.astype(BF16), q_gain.reshape(1, XA_DIM),
      mem_k, mem_v, wo.astype(BF16))


def _ffn_kernel(h_ref, g_ref, wup_ref, cw_ref, cb_ref, wdn_ref, out_ref, carry_ref, act_ref,
                *, per_seq, tm):
    @pl.when(pl.program_id(0) % per_seq == 0)
    def _():
        carry_ref[...] = jnp.zeros_like(carry_ref)

    h = h_ref[...]
    hn = _rms(h, g_ref[...]).astype(BF16)

    def up_conv(slot, col0):
        cols = slice(col0, col0 + FF_CHUNK)
        u = _dot(hn, wup_ref[:, cols])
        xc = jnp.concatenate([carry_ref[slot], u], axis=0)
        carry_ref[slot] = u[tm - 8:]
        s1 = pltpu.roll(xc, 1, 0)[8:]
        s2 = pltpu.roll(xc, 2, 0)[8:]
        return s2 * cw_ref[0:1, cols] + s1 * cw_ref[1:2, cols] + u * cw_ref[2:3, cols] + cb_ref[:, cols]

    for c in range(N_FF):
        gate = up_conv(2 * c, c * FF_CHUNK)
        val = up_conv(2 * c + 1, D_FF + c * FF_CHUNK)
        act_ref[:, c * FF_CHUNK:(c + 1) * FF_CHUNK] = (_silu(gate) * val).astype(BF16)
    out_ref[...] = h + _dot(act_ref[...], wdn_ref[...])


def _conv_ffn(bsz, seq, h, gain, w_up, conv_w, conv_b, w_down):
    d = D_MODEL
    tokens = bsz * seq
    tm = min(seq, FFN_TOKENS)
    per_seq = seq // tm
    wup = w_up.astype(BF16)
    cw = conv_w
    cb = conv_b.reshape(1, -1)
    wdn = w_down.astype(BF16)
    const = lambda shape: pl.BlockSpec(shape, lambda i: (0,) * len(shape), pipeline_mode=pl.Buffered(1))
    out = pl.pallas_call(
        functools.partial(_ffn_kernel, per_seq=per_seq, tm=tm),
        grid=(tokens // tm,),
        in_specs=[pl.BlockSpec((tm, d), lambda i: (i, 0)), const((1, d)),
                  const(wup.shape), const(cw.shape), const(cb.shape), const(wdn.shape)],
        out_specs=pl.BlockSpec((tm, d), lambda i: (i, 0)),
        out_shape=jax.ShapeDtypeStruct((tokens, d), F32),
        scratch_shapes=[pltpu.VMEM((2 * N_FF, 8, FF_CHUNK), F32), pltpu.VMEM((tm, D_FF), BF16)],
        compiler_params=_cparams("arbitrary"),
    )(h, gain.reshape(1, d), wup, cw, cb, wdn)
    return out


def _s5_disc_kernel(lr_ref, li_ref, ldt_ref, bre_ref, bim_ref, are_ref, aim_ref, bbr_ref, bbi_ref):
    lr = lr_ref[...]
    li = li_ref[...]
    dt = jnp.exp(ldt_ref[...])
    mag = jnp.exp(lr * dt)
    ab_re = mag * jnp.cos(li * dt)
    ab_im = mag * jnp.sin(li * dt)
    den = lr * lr + li * li
    z_re = ((ab_re - 1.0) * lr + ab_im * li) / den
    z_im = (ab_im * lr - (ab_re - 1.0) * li) / den
    br = bre_ref[...]
    bi = bim_ref[...]
    are_ref[...] = ab_re
    aim_ref[...] = ab_im
    bbr_ref[...] = z_re * br - z_im * bi
    bbi_ref[...] = z_re * bi + z_im * br


def _s5_discretise(lam_re, lam_im, log_dt, b_re, b_im):
    n = S5_GROUPS * S5_STATE
    col = lambda a: a.reshape(n, 1)
    ldt = jnp.repeat(log_dt, S5_STATE).reshape(n, 1)
    vec = jax.ShapeDtypeStruct((n, 1), F32)
    mat = jax.ShapeDtypeStruct((n, S5_GROUP), F32)
    return pl.pallas_call(_s5_disc_kernel, out_shape=[vec, vec, mat, mat])(
        col(lam_re), col(lam_im), ldt, b_re.reshape(n, S5_GROUP), b_im.reshape(n, S5_GROUP))


def _s5_kernel(h_ref, g_ref, wbu_ref, are_ref, aim_ref, wcr_ref, wci_ref, d_ref, y_ref,
               st_ref, hs_ref, *, tm, bsz):
    half = 8 * S5_STATE
    nl = half // 128
    nt = tm // bsz

    @pl.when(pl.program_id(0) == 0)
    def _():
        st_ref[...] = jnp.zeros_like(st_ref)

    hn = _rms(pltpu.einshape("btd->tbd", h_ref[...]).reshape(tm, D_MODEL), g_ref[...])
    u = hn.astype(BF16)
    ys = []
    for j in range(S5_BLOCKS):
        buf = hs_ref.at[j % 2]
        buf[...] = _dot(u[:, j * 128:(j + 1) * 128], wbu_ref[j])
        ar = [jnp.broadcast_to(are_ref[j, :, c * 128:(c + 1) * 128], (bsz, 128)) for c in range(nl)]
        ai = [jnp.broadcast_to(aim_ref[j, :, c * 128:(c + 1) * 128], (bsz, 128)) for c in range(nl)]
        state = [st_ref[:, j * 2 * half + c * 128:j * 2 * half + (c + 1) * 128] for c in range(2 * nl)]
        for t in range(nt):
            rows = slice(t * bsz, (t + 1) * bsz)
            for c in range(nl):
                re_l = slice(c * 128, (c + 1) * 128)
                im_l = slice(half + c * 128, half + (c + 1) * 128)
                hr, hi = state[c], state[nl + c]
                nr = ar[c] * hr - ai[c] * hi + buf[rows, re_l]
                ni = ar[c] * hi + ai[c] * hr + buf[rows, im_l]
                buf[rows, re_l] = nr
                buf[rows, im_l] = ni
                state[c], state[nl + c] = nr, ni
        for c in range(2 * nl):
            st_ref[:, j * 2 * half + c * 128:j * 2 * half + (c + 1) * 128] = state[c]
        hb = buf[...].astype(BF16)
        ys.append(_dot(hb[:, :half], wcr_ref[j]) - _dot(hb[:, half:], wci_ref[j]))
    y = jax.nn.gelu(jnp.concatenate(ys, axis=-1) + d_ref[...] * hn)
    y_ref[...] = pltpu.einshape("tbd->btd", y.reshape(nt, bsz, D_MODEL)).astype(y_ref.dtype)


def _s5_mixer_core(bsz, seq, h, gain, lam_re, lam_im, log_dt, b_re, b_im, c_re, c_im, d_skip):
    d = D_MODEL
    assert bsz == 8, "the S5 scan keeps one batch element per sublane"
    ab_re, ab_im, bb_re, bb_im = _s5_discretise(lam_re, lam_im, log_dt, b_re, b_im)
    nb, gl, p, m = S5_BLOCKS, S5_GROUPS // S5_BLOCKS, S5_STATE, S5_GROUP
    eye = jnp.eye(gl, dtype=F32)

    def bu_weight(bb):
        bb = bb.reshape(nb, gl, p, m).transpose(0, 1, 3, 2)
        return (bb[:, :, :, None, :] * eye[None, :, None, :, None]).reshape(nb, gl * m, gl * p)

    def c_weight(cc):
        cc = cc.reshape(nb, gl, m, p).transpose(0, 1, 3, 2)
        return (cc[:, :, :, None, :] * eye[None, :, None, :, None]).reshape(nb, gl * p, gl * m)

    wbu = jnp.concatenate([bu_weight(bb_re), bu_weight(bb_im)], axis=-1).astype(BF16)
    wcr = c_weight(c_re).astype(BF16)
    wci = c_weight(c_im).astype(BF16)
    a_re = ab_re.reshape(nb, 1, gl * p)
    a_im = ab_im.reshape(nb, 1, gl * p)

    nt = min(seq, 64)
    tm = bsz * nt
    const = lambda shape: pl.BlockSpec(shape, lambda i: (0,) * len(shape))
    return pl.pallas_call(
        functools.partial(_s5_kernel, tm=tm, bsz=bsz),
        grid=(seq // nt,),
        in_specs=[pl.BlockSpec((bsz, nt, d), lambda i: (0, i, 0)), const((1, d)), const(wbu.shape),
                  const(a_re.shape), const(a_im.shape), const(wcr.shape), const(wci.shape),
                  const((1, d))],
        out_specs=pl.BlockSpec((bsz, nt, d), lambda i: (0, i, 0)),
        out_shape=jax.ShapeDtypeStruct((bsz, seq, d), BF16),
        scratch_shapes=[pltpu.VMEM((bsz, 2 * S5_GROUPS * S5_STATE), F32),
                        pltpu.VMEM((2, tm, 2 * gl * p), F32)],
        compiler_params=_cparams("arbitrary"),
    )(h.reshape(bsz, seq, d), gain.reshape(1, d), wbu, a_re, a_im, wcr, wci,
      d_skip.reshape(1, d)).reshape(bsz * seq, d)


def kernel(x, mem, positions, norm_mix, norm_xa, norm_mem, norm_ffn, xa_wq, xa_wk, xa_wv, xa_wo, xa_q_norm, xa_k_norm, ffn_w_up, ffn_conv_w, ffn_conv_b, ffn_w_down, hg_lb_logits, mix_w_in, hg_out_norm, mla_q_a_norm, mla_w_uq, mla_kv_a_norm, mla_w_ukv, mla_qn_nope, mla_qn_rope, mla_kn_nope, mla_kn_rope, mix_w_out, s5_lam_re, s5_lam_im, s5_log_dt, s5_b_re, s5_b_im, s5_c_re, s5_c_im, s5_d, s5_w_glu_a, s5_w_glu_b):
    depth = norm_mix.shape[0]
    ctab, stab = _rope_tables(positions)
    mem_k, mem_v = _memory_kv(mem, norm_mem, xa_wk, xa_wv, xa_k_norm)
    bsz, seq, d = x.shape
    tokens = bsz * seq
    h = x.reshape(tokens, d)
    for layer in range(depth):
        j = layer // 2
        if layer % 2 == 0:
            use_bounded, shift = _score_shift(mla_qn_nope[j], mla_qn_rope[j], mla_kn_nope[j], mla_kn_rope[j])
            hg, q, k, v = _mix_prep(h.reshape(bsz, seq, d), norm_mix[layer], mix_w_in[j], mla_q_a_norm[j],
                                    mla_w_uq[j], mla_kv_a_norm[j], mla_w_ukv[j], mla_qn_nope[j],
                                    mla_qn_rope[j], mla_kn_nope[j], mla_kn_rope[j], shift, ctab, stab)
            o_hg = _hgrn(hg, hg_lb_logits, j, hg_out_norm[j]).reshape(tokens, HG_WIDTH)
            o_mla = lax.cond(use_bounded,
                             functools.partial(_mla_attention, bounded=True),
                             functools.partial(_mla_attention, bounded=False),
                             q, k, v).reshape(tokens, MLA_HEADS * MLA_V)
            h = _post_mixer(_post_proj_kernel, bsz, seq, h, [o_hg, o_mla], [mix_w_out[j].astype(BF16)],
                            norm_xa[layer], xa_wq[layer], xa_q_norm[layer], mem_k[layer], mem_v[layer],
                            xa_wo[layer])
        else:
            y = _s5_mixer_core(bsz, seq, h, norm_mix[layer], s5_lam_re[j], s5_lam_im[j], s5_log_dt[j],
                               s5_b_re[j], s5_b_im[j], s5_c_re[j], s5_c_im[j], s5_d[j])
            h = _post_mixer(_post_glu_kernel, bsz, seq, h, [y],
                            [s5_w_glu_a[j].astype(BF16), s5_w_glu_b[j].astype(BF16)],
                            norm_xa[layer], xa_wq[layer], xa_q_norm[layer], mem_k[layer], mem_v[layer],
                            xa_wo[layer])
        h = _conv_ffn(bsz, seq, h, norm_ffn[layer], ffn_w_up[layer], ffn_conv_w[layer],
                      ffn_conv_b[layer], ffn_w_down[layer])
    return h.reshape(bsz, seq, d)
```

```python
import functools

import numpy as np
import jax
import jax.numpy as jnp
from jax import lax
from jax.experimental import pallas as pl
from jax.experimental.pallas import tpu as pltpu

F32 = jnp.float32
BF16 = jnp.bfloat16
EPS = 1e-6
NEG_BIG = -1e30
LOG2_E = 1.4426950408889634

D_MODEL = 1024
HG_HEADS = 4
HG_DIM = 128
HG_WIDTH = HG_HEADS * HG_DIM
HG_CHUNK = 64
HG_LEVELS = (32, 16, 8, 4, 2, 1)
MLA_HEADS = 4
MLA_Q_RANK = 256
MLA_KV_RANK = 128
MLA_NOPE = 128
MLA_ROPE = 64
MLA_V = 128
MLA_QK = MLA_NOPE + MLA_ROPE
MLA_QK_PAD = 256
ATTN_HEADS_PER_STEP = 4
ATTN_MAX_SHIFT = 60.0
FFN_TOKENS = 1024
POST_TOKENS = 1024
PREP_ROW_SPLIT = 2
ROPE_BASE = 10000.0
S5_GROUP = 16
S5_GROUPS = D_MODEL // S5_GROUP
S5_STATE = 64
S5_BLOCKS = 8
XA_HEADS = 4
XA_DIM = D_MODEL // XA_HEADS
D_FF = 2816
FF_CHUNK = 256
N_FF = D_FF // FF_CHUNK
IN_PAD = 4 * HG_WIDTH + MLA_Q_RANK + MLA_KV_RANK + 128

VMEM_LIMIT_BYTES = 56 * 1024 * 1024


def _cparams(*sem):
    return pltpu.CompilerParams(dimension_semantics=sem, vmem_limit_bytes=VMEM_LIMIT_BYTES)


def _dot(a, b):
    return jnp.dot(a, b, preferred_element_type=F32)


def _dot_nt(a, b):
    return lax.dot_general(a, b, (((1,), (1,)), ((), ())), preferred_element_type=F32)


def _rms(x, gain, width=None):
    n = x.shape[-1] if width is None else width
    ms = jnp.sum(x * x, axis=-1, keepdims=True) * (1.0 / n)
    return x * lax.rsqrt(ms + EPS) * gain


def _sigmoid(x):
    return 0.5 * jnp.tanh(0.5 * x) + 0.5


def _silu(x):
    return x * _sigmoid(x)


def _rope_kernel(pos_ref, invf_ref, cos_ref, sin_ref):
    ang = pos_ref[...].astype(F32) * invf_ref[...]
    cos_ref[...] = jnp.cos(ang)
    sin_ref[...] = jnp.sin(ang)


def _rope_tables(positions):
    bsz, seq = positions.shape
    half = MLA_ROPE // 2
    inv_freq = 1.0 / (ROPE_BASE ** (jnp.arange(0, MLA_ROPE, 2, dtype=F32) / MLA_ROPE))
    rep = 128 // half
    rows = bsz * seq // rep
    pos_rep = jnp.broadcast_to(positions.reshape(rows, rep, 1), (rows, rep, half)).reshape(rows, 128)
    invf = jnp.tile(inv_freq, rep).reshape(1, 128)
    tr = min(rows, 1024)
    cos, sin = pl.pallas_call(
        _rope_kernel,
        grid=(rows // tr,),
        in_specs=[pl.BlockSpec((tr, 128), lambda i: (i, 0)),
                  pl.BlockSpec((1, 128), lambda i: (0, 0))],
        out_specs=[pl.BlockSpec((tr, 128), lambda i: (i, 0))] * 2,
        out_shape=[jax.ShapeDtypeStruct((rows, 128), F32)] * 2,
        compiler_params=_cparams("parallel"),
    )(pos_rep, invf)
    cos = cos.reshape(bsz, seq, half)
    sin = sin.reshape(bsz, seq, half)
    zeros = jnp.zeros((bsz, seq, half), F32)
    ctab = jnp.concatenate([cos, zeros, cos, zeros], axis=-1)
    stab = jnp.concatenate([-sin, zeros, sin, zeros], axis=-1)
    return ctab, stab


def _spread_rope(a):
    half = MLA_ROPE // 2
    a = a.reshape(*a.shape[:-1], 2, half)
    a = jnp.pad(a, [(0, 0)] * (a.ndim - 1) + [(0, half)])
    return a.reshape(*a.shape[:-2], 4 * half)


def _memkv_kernel(mem_ref, g_ref, wk_ref, wv_ref, kg_ref, k_ref, v_ref):
    m = _rms(mem_ref[0], g_ref[0]).astype(BF16)
    k = _dot(m, wk_ref[0])
    v = _dot(m, wv_ref[0])
    for h in range(XA_HEADS):
        sl = slice(h * XA_DIM, (h + 1) * XA_DIM)
        k_ref[0, 0, :, sl] = _rms(k[:, sl], kg_ref[0]).astype(BF16)
    v_ref[0, 0] = v.astype(BF16)


def _memory_kv(mem, norm_mem, wk, wv, k_gain):
    depth = norm_mem.shape[0]
    bsz, mlen, d = mem.shape
    out = jax.ShapeDtypeStruct((depth, bsz, mlen, d), BF16)
    return pl.pallas_call(
        _memkv_kernel,
        grid=(depth, bsz),
        in_specs=[pl.BlockSpec((1, mlen, d), lambda l, b: (b, 0, 0)),
                  pl.BlockSpec((1, 1, d), lambda l, b: (l, 0, 0)),
                  pl.BlockSpec((1, d, d), lambda l, b: (l, 0, 0)),
                  pl.BlockSpec((1, d, d), lambda l, b: (l, 0, 0)),
                  pl.BlockSpec((1, 1, XA_DIM), lambda l, b: (l, 0, 0))],
        out_specs=[pl.BlockSpec((1, 1, mlen, d), lambda l, b: (l, b, 0, 0))] * 2,
        out_shape=[out, out],
        compiler_params=_cparams("parallel", "parallel"),
    )(mem, norm_mem.reshape(depth, 1, d), wk.astype(BF16), wv.astype(BF16),
      k_gain.reshape(depth, 1, XA_DIM))


def _mix_prep_kernel(h_ref, gain_ref, win_ref, qa_ref, wuq_ref, kva_ref, wukv_ref,
                     qnn_ref, qnr_ref, knn_ref, knr_ref, qone_ref, kshift_ref, ctab_ref, stab_ref,
                     hg_ref, q_ref, k_ref, v_ref, *, tl):
    scale = MLA_QK ** -0.5 * LOG2_E
    ts = tl // PREP_ROW_SPLIT
    for r in range(PREP_ROW_SPLIT):
        rows = slice(r * ts, (r + 1) * ts)
        hn = _rms(h_ref[0, rows, :], gain_ref[...]).astype(BF16)
        proj = _dot(hn, win_ref[...])
        hg_ref[0, rows, :] = proj[:, :4 * HG_WIDTH]
        o = 4 * HG_WIDTH
        c_q = proj[:, o:o + MLA_Q_RANK]
        c_kv = proj[:, o + MLA_Q_RANK:o + MLA_Q_RANK + MLA_KV_RANK]
        k_pe = proj[:, o + MLA_Q_RANK + MLA_KV_RANK:]
        q = _dot(_rms(c_q, qa_ref[...]).astype(BF16), wuq_ref[...])
        kv = _dot(_rms(c_kv, kva_ref[...]).astype(BF16), wukv_ref[...])
        ctab = ctab_ref[0, rows, :]
        stab = stab_ref[0, rows, :]

        def rope(x):
            return x * ctab + pltpu.roll(x, 64, 1) * stab

        k_rope = (rope(_rms(k_pe, knr_ref[...], MLA_ROPE)) + kshift_ref[...]).astype(BF16)
        for h in range(MLA_HEADS):
            b0 = h * MLA_QK_PAD
            qn = _rms(q[:, b0:b0 + MLA_NOPE], qnn_ref[...])
            qr = rope(_rms(q[:, b0 + MLA_NOPE:b0 + MLA_QK_PAD], qnr_ref[...], MLA_ROPE))
            q_ref[0, h, rows, :MLA_NOPE] = (qn * scale).astype(BF16)
            q_ref[0, h, rows, MLA_NOPE:] = (qr * scale + qone_ref[...]).astype(BF16)
            kn = _rms(kv[:, b0:b0 + MLA_NOPE], knn_ref[...])
            k_ref[0, h, rows, :MLA_NOPE] = kn.astype(BF16)
            k_ref[0, h, rows, MLA_NOPE:] = k_rope
            v_ref[0, h, rows, :] = kv[:, b0 + MLA_NOPE:b0 + MLA_NOPE + MLA_V].astype(BF16)


def _mix_prep(h, gain, w_in, q_a_norm, w_uq, kv_a_norm, w_ukv, qn_nope, qn_rope, kn_nope, kn_rope,
              shift, ctab, stab):
    bsz, seq, d = h.shape
    spare = np.zeros((1, 128), np.float32)
    spare[0, MLA_ROPE // 2] = 1.0
    q_one = jnp.asarray(spare)
    k_shift = q_one * (-shift)
    tl = min(seq, 512)
    n_hg = 4 * HG_WIDTH + MLA_Q_RANK + MLA_KV_RANK
    w_in_p = jnp.concatenate([w_in[:, :n_hg], _spread_rope(w_in[:, n_hg:])], axis=1).astype(BF16)
    w_uq_h = w_uq.reshape(MLA_Q_RANK, MLA_HEADS, MLA_QK)
    w_uq_p = jnp.concatenate([w_uq_h[..., :MLA_NOPE], _spread_rope(w_uq_h[..., MLA_NOPE:])],
                             axis=-1).reshape(MLA_Q_RANK, -1).astype(BF16)
    pad64 = lambda g: _spread_rope(g).reshape(1, 128)
    row = lambda g: g.reshape(1, -1)
    const = lambda shape: pl.BlockSpec(shape, lambda b, i: (0,) * len(shape))
    hw = MLA_HEADS
    return pl.pallas_call(
        functools.partial(_mix_prep_kernel, tl=tl),
        grid=(bsz, seq // tl),
        in_specs=[pl.BlockSpec((1, tl, d), lambda b, i: (b, i, 0)),
                  const((1, d)), const((d, IN_PAD)),
                  const((1, MLA_Q_RANK)), const((MLA_Q_RANK, hw * MLA_QK_PAD)),
                  const((1, MLA_KV_RANK)), const((MLA_KV_RANK, hw * (MLA_NOPE + MLA_V))),
                  const((1, MLA_NOPE)), const((1, 128)), const((1, MLA_NOPE)), const((1, 128)),
                  const((1, 128)), const((1, 128)),
                  pl.BlockSpec((1, tl, 128), lambda b, i: (b, i, 0)),
                  pl.BlockSpec((1, tl, 128), lambda b, i: (b, i, 0))],
        out_specs=[pl.BlockSpec((1, tl, 4 * HG_WIDTH), lambda b, i: (b, i, 0)),
                   pl.BlockSpec((1, hw, tl, MLA_QK_PAD), lambda b, i: (b, 0, i, 0)),
                   pl.BlockSpec((1, hw, tl, MLA_QK_PAD), lambda b, i: (b, 0, i, 0)),
                   pl.BlockSpec((1, hw, tl, MLA_V), lambda b, i: (b, 0, i, 0))],
        out_shape=[jax.ShapeDtypeStruct((bsz, seq, 4 * HG_WIDTH), F32),
                   jax.ShapeDtypeStruct((bsz, hw, seq, MLA_QK_PAD), BF16),
                   jax.ShapeDtypeStruct((bsz, hw, seq, MLA_QK_PAD), BF16),
                   jax.ShapeDtypeStruct((bsz, hw, seq, MLA_V), BF16)],
        compiler_params=_cparams("parallel", "parallel"),
    )(h, row(gain), w_in_p, row(q_a_norm), w_uq_p, row(kv_a_norm), w_ukv.astype(BF16),
      row(qn_nope), pad64(qn_rope), row(kn_nope), pad64(kn_rope), q_one, k_shift, ctab, stab)


def _hgrn_tables():
    m = np.tril(np.ones((HG_CHUNK, HG_CHUNK), np.float32))
    return np.concatenate([m, m, m], axis=1)


def _hgrn_level_reference(b, s, row_in_tile):
    c, n = b.shape
    if s >= 4:
        return jnp.concatenate([jnp.broadcast_to(b[r0 + s - 1:r0 + s], (2 * s, n))
                                for r0 in range(0, c, 2 * s)], axis=0)
    if s == 2:
        first = jnp.concatenate([jnp.broadcast_to(b[r0 + 1:r0 + 2], (8, n)) for r0 in range(0, c, 8)], axis=0)
        second = jnp.concatenate([jnp.broadcast_to(b[r0 + 5:r0 + 6], (8, n)) for r0 in range(0, c, 8)], axis=0)
        return jnp.where(row_in_tile < 4, first, second)
    assert s == 1
    return jnp.where(row_in_tile % 2 == 1, pltpu.roll(b, 1, 0), b)


def _hgrn_kernel(hg_ref, lbl_ref, gn_ref, m3_ref, o_ref, st_ref, *, tb, layer_index):
    c = HG_CHUNK

    @pl.when(pl.program_id(1) == 0)
    def _():
        st_ref[...] = jnp.zeros_like(st_ref)

    lg = lbl_ref[...]
    eg = jnp.exp(lg - jnp.max(lg, axis=0, keepdims=True))
    lb_row = jnp.sum(eg[:layer_index + 1], axis=0, keepdims=True) / jnp.sum(eg, axis=0, keepdims=True)

    ti = lax.broadcasted_iota(jnp.int32, (c, c), 0)
    si = lax.broadcasted_iota(jnp.int32, (c, c), 1)
    xor = ti ^ si
    lower = ti > si
    level_masks = [lower & (xor >= s) & (xor < 2 * s) for s in HG_LEVELS]
    on_diag = ti == si
    row = lax.broadcasted_iota(jnp.int32, (c, HG_DIM), 0)
    row_in_tile = row % 8
    level_signs = [jnp.where((row & s) != 0, LOG2_E, -LOG2_E) for s in HG_LEVELS]
    m3 = m3_ref[...]

    states = [st_ref[h] for h in range(HG_HEADS)]
    for ci in range(tb // c):
        rows = slice(ci * c, (ci + 1) * c)
        for h in range(HG_HEADS):
            lane = slice(h * HG_DIM, (h + 1) * HG_DIM)
            q = hg_ref[0, rows, h * HG_DIM:(h + 1) * HG_DIM]
            fl = hg_ref[0, rows, HG_WIDTH + h * HG_DIM:HG_WIDTH + (h + 1) * HG_DIM]
            iv = hg_ref[0, rows, 2 * HG_WIDTH + h * HG_DIM:2 * HG_WIDTH + (h + 1) * HG_DIM]
            g = hg_ref[0, rows, 3 * HG_WIDTH + h * HG_DIM:3 * HG_WIDTH + (h + 1) * HG_DIM]
            lb = lb_row[:, lane]
            f = lb + (1.0 - lb) * _sigmoid(fl)
            lf = jnp.log(f)
            kk = 1.0 - f
            qf = _silu(q)
            hi = lf.astype(BF16)
            r1 = lf - hi.astype(F32)
            mid = r1.astype(BF16)
            lo = (r1 - mid.astype(F32)).astype(BF16)
            b = _dot(m3, jnp.concatenate([hi, mid, lo], axis=0))
            vb = iv.astype(BF16)
            sc = jnp.where(on_diag, jnp.sum(qf * kk, axis=-1, keepdims=True), 0.0)
            for li, s in enumerate(HG_LEVELS):
                el = jnp.exp2((b - _hgrn_level_reference(b, s, row_in_tile)) * level_signs[li])
                sc = jnp.where(level_masks[li],
                               _dot_nt((qf * el).astype(BF16), (kk * el).astype(BF16)), sc)
            eb = jnp.exp(b)
            ee = jnp.exp(b[c - 1:c, :] - b)
            st = states[h]
            o = _dot(sc.astype(BF16), vb) + _dot_nt((qf * eb).astype(BF16), st.astype(BF16))
            kend = (kk * ee).astype(BF16)
            states[h] = st * eb[c - 1:c, :] + _dot(iv.T.astype(BF16), kend)
            on = _rms(o, gn_ref[:, lane]) * _silu(g)
            o_ref[0, rows, h * HG_DIM:(h + 1) * HG_DIM] = on.astype(o_ref.dtype)
    for h in range(HG_HEADS):
        st_ref[h] = states[h]


def _hgrn(hg, lb_logits, layer_index, out_norm):
    bsz, seq, _ = hg.shape
    tb = min(seq, 512)
    m3 = jnp.asarray(_hgrn_tables(), BF16)
    return pl.pallas_call(
        functools.partial(_hgrn_kernel, tb=tb, layer_index=layer_index),
        grid=(bsz, seq // tb),
        in_specs=[pl.BlockSpec((1, tb, 4 * HG_WIDTH), lambda b, i: (b, i, 0)),
                  pl.BlockSpec(lb_logits.shape, lambda b, i: (0, 0)),
                  pl.BlockSpec((1, HG_WIDTH), lambda b, i: (0, 0)),
                  pl.BlockSpec(m3.shape, lambda b, i: (0, 0))],
        out_specs=pl.BlockSpec((1, tb, HG_WIDTH), lambda b, i: (b, i, 0)),
        out_shape=jax.ShapeDtypeStruct((bsz, seq, HG_WIDTH), BF16),
        scratch_shapes=[pltpu.VMEM((HG_HEADS, HG_DIM, HG_DIM), F32)],
        compiler_params=_cparams("parallel", "arbitrary"),
    )(hg, lb_logits, out_norm.reshape(1, HG_WIDTH), m3)


def _mla_attn_kernel(q_ref, k_ref, v_ref, o_ref, *, tq, tk, bounded):
    assert tq == tk
    qi = pl.program_id(2)
    heads = q_ref.shape[1]
    on_or_below_diag = (lax.broadcasted_iota(jnp.int32, (tq, tk), 1)
                        <= lax.broadcasted_iota(jnp.int32, (tq, tk), 0))

    def step(j, carry, on_diag):
        keys = pl.ds(pl.multiple_of(j * tk, tk), tk)
        out = []
        for h in range(heads):
            s = _dot_nt(q_ref[0, h], k_ref[0, h, keys, :])
            if on_diag:
                s = jnp.where(on_or_below_diag, s, NEG_BIG)
            if bounded:
                l, acc = carry[h]
                p = jnp.exp2(s)
                out.append((l + jnp.sum(p, axis=-1, keepdims=True),
                            acc + _dot(p.astype(BF16), v_ref[0, h, keys, :])))
            else:
                m, l, acc = carry[h]
                m_new = jnp.maximum(m, jnp.max(s, axis=-1, keepdims=True))
                alpha = jnp.exp2(m - m_new)
                p = jnp.exp2(s - m_new)
                out.append((m_new, alpha * l + jnp.sum(p, axis=-1, keepdims=True),
                            alpha * acc + _dot(p.astype(BF16), v_ref[0, h, keys, :])))
        return tuple(out)

    init = (jnp.zeros((tq, 1), F32), jnp.zeros((tq, MLA_V), F32))
    if not bounded:
        init = (jnp.full((tq, 1), NEG_BIG, F32),) + init
    carry = lax.fori_loop(0, qi, lambda j, cr: step(j, cr, False), (init,) * heads)
    carry = step(qi, carry, True)
    for h in range(heads):
        o_ref[0, :, h * MLA_V:(h + 1) * MLA_V] = (carry[h][-1] / carry[h][-2]).astype(o_ref.dtype)


def _mla_attention(q, k, v, bounded):
    bsz, heads, seq, _ = q.shape
    tq = min(seq, 512)
    tk = min(seq, 512)
    hg = ATTN_HEADS_PER_STEP
    return pl.pallas_call(
        functools.partial(_mla_attn_kernel, tq=tq, tk=tk, bounded=bounded),
        grid=(bsz, heads // hg, seq // tq),
        in_specs=[pl.BlockSpec((1, hg, tq, MLA_QK_PAD), lambda b, h, i: (b, h, i, 0)),
                  pl.BlockSpec((1, hg, seq, MLA_QK_PAD), lambda b, h, i: (b, h, 0, 0)),
                  pl.BlockSpec((1, hg, seq, MLA_V), lambda b, h, i: (b, h, 0, 0))],
        out_specs=pl.BlockSpec((1, tq, hg * MLA_V), lambda b, h, i: (b, i, h)),
        out_shape=jax.ShapeDtypeStruct((bsz, seq, heads * MLA_V), BF16),
        compiler_params=_cparams("parallel", "parallel", "parallel"),
    )(q, k, v)


def _score_shift(qn_nope, qn_rope, kn_nope, kn_rope):
    def max_norm2(g_nope, g_rope):
        return MLA_NOPE * jnp.max(g_nope * g_nope) + MLA_ROPE * jnp.max(g_rope * g_rope)

    bound = MLA_QK ** -0.5 * LOG2_E * jnp.sqrt(max_norm2(qn_nope, qn_rope) * max_norm2(kn_nope, kn_rope))
    shift = jnp.ceil(1.02 * bound + 1.0)
    use_bounded = shift <= ATTN_MAX_SHIFT
    return use_bounded, jnp.where(use_bounded, shift, 0.0)


def _cross_attention(h1, nx_ref, wq_ref, qg_ref, mk_ref, mv_ref, wo_ref):
    hn = _rms(h1, nx_ref[...]).astype(BF16)
    q = _dot(hn, wq_ref[...])
    outs = []
    for hd in range(XA_HEADS):
        sl = slice(hd * XA_DIM, (hd + 1) * XA_DIM)
        qh = (_rms(q[:, sl], qg_ref[...]) * (XA_DIM ** -0.5)).astype(BF16)
        s = _dot_nt(qh, mk_ref[0, :, sl])
        e = jnp.exp(s - jnp.max(s, axis=-1, keepdims=True))
        p = e / jnp.sum(e, axis=-1, keepdims=True)
        outs.append(_dot(p.astype(BF16), mv_ref[0, :, sl]).astype(BF16))
    return h1 + _dot(jnp.concatenate(outs, axis=-1), wo_ref[...])


def _post_proj_kernel(h_ref, ohg_ref, omla_ref, wout_ref, nx_ref, wq_ref, qg_ref, mk_ref, mv_ref,
                      wo_ref, out_ref):
    o = jnp.concatenate([ohg_ref[...], omla_ref[...]], axis=-1)
    h1 = h_ref[...] + _dot(o, wout_ref[...])
    out_ref[...] = _cross_attention(h1, nx_ref, wq_ref, qg_ref, mk_ref, mv_ref, wo_ref)


def _post_glu_kernel(h_ref, y_ref, wa_ref, wb_ref, nx_ref, wq_ref, qg_ref, mk_ref, mv_ref,
                     wo_ref, out_ref):
    y = y_ref[...]
    h1 = h_ref[...] + _dot(y, wa_ref[...]) * _sigmoid(_dot(y, wb_ref[...]))
    out_ref[...] = _cross_attention(h1, nx_ref, wq_ref, qg_ref, mk_ref, mv_ref, wo_ref)


def _post_mixer(kernel_fn, bsz, seq, h, acts, mix_weights, norm_xa, wq, q_gain, mem_k, mem_v, wo):
    d = D_MODEL
    tokens = bsz * seq
    tm = min(seq, POST_TOKENS)
    per_seq = seq // tm
    mlen = mem_k.shape[1]
    rows = lambda w: pl.BlockSpec((tm, w), lambda i: (i, 0))
    const = lambda shape: pl.BlockSpec(shape, lambda i: (0,) * len(shape))
    mem_spec = pl.BlockSpec((1, mlen, d), lambda i: (i // per_seq, 0, 0))
    return pl.pallas_call(
        kernel_fn,
        grid=(tokens // tm,),
        in_specs=([rows(d)] + [rows(a.shape[-1]) for a in acts] + [const(w.shape) for w in mix_weights]
                  + [const((1, d)), const((d, d)), const((1, XA_DIM)), mem_spec, mem_spec, const((d, d))]),
        out_specs=rows(d),
        out_shape=jax.ShapeDtypeStruct((tokens, d), F32),
        compiler_params=_cparams("parallel"),
    )(h, *acts, *mix_weights, norm_xa.reshape(1, d), wq.astype(BF16), q_gain.reshape(1, XA_DIM),
      mem_k, mem_v, wo.astype(BF16))


def _ffn_kernel(h_ref, g_ref, wup_ref, cw_ref, cb_ref, wdn_ref, out_ref, carry_ref, act_ref,
                *, per_seq, tm):
    @pl.when(pl.program_id(0) % per_seq == 0)
    def _():
        carry_ref[...] = jnp.zeros_like(carry_ref)

    h = h_ref[...]
    hn = _rms(h, g_ref[...]).astype(BF16)

    def up_conv(slot, col0):
        cols = slice(col0, col0 + FF_CHUNK)
        u = _dot(hn, wup_ref[:, cols])
        xc = jnp.concatenate([carry_ref[slot], u], axis=0)
        carry_ref[slot] = u[tm - 8:]
        s1 = pltpu.roll(xc, 1, 0)[8:]
        s2 = pltpu.roll(xc, 2, 0)[8:]
        return s2 * cw_ref[0:1, cols] + s1 * cw_ref[1:2, cols] + u * cw_ref[2:3, cols] + cb_ref[:, cols]

    for c in range(N_FF):
        gate = up_conv(2 * c, c * FF_CHUNK)
        val = up_conv(2 * c + 1, D_FF + c * FF_CHUNK)
        act_ref[:, c * FF_CHUNK:(c + 1) * FF_CHUNK] = (_silu(gate) * val).astype(BF16)
    out_ref[...] = h + _dot(act_ref[...], wdn_ref[...])


def _conv_ffn(bsz, seq, h, gain, w_up, conv_w, conv_b, w_down):
    d = D_MODEL
    tokens = bsz * seq
    tm = min(seq, FFN_TOKENS)
    per_seq = seq // tm
    wup = w_up.astype(BF16)
    cw = conv_w
    cb = conv_b.reshape(1, -1)
    wdn = w_down.astype(BF16)
    const = lambda shape: pl.BlockSpec(shape, lambda i: (0,) * len(shape), pipeline_mode=pl.Buffered(1))
    out = pl.pallas_call(
        functools.partial(_ffn_kernel, per_seq=per_seq, tm=tm),
        grid=(tokens // tm,),
        in_specs=[pl.BlockSpec((tm, d), lambda i: (i, 0)), const((1, d)),
                  const(wup.shape), const(cw.shape), const(cb.shape), const(wdn.shape)],
        out_specs=pl.BlockSpec((tm, d), lambda i: (i, 0)),
        out_shape=jax.ShapeDtypeStruct((tokens, d), F32),
        scratch_shapes=[pltpu.VMEM((2 * N_FF, 8, FF_CHUNK), F32), pltpu.VMEM((tm, D_FF), BF16)],
        compiler_params=_cparams("arbitrary"),
    )(h, gain.reshape(1, d), wup, cw, cb, wdn)
    return out


def _s5_disc_kernel(lr_ref, li_ref, ldt_ref, bre_ref, bim_ref, are_ref, aim_ref, bbr_ref, bbi_ref):
    lr = lr_ref[...]
    li = li_ref[...]
    dt = jnp.exp(ldt_ref[...])
    mag = jnp.exp(lr * dt)
    ab_re = mag * jnp.cos(li * dt)
    ab_im = mag * jnp.sin(li * dt)
    den = lr * lr + li * li
    z_re = ((ab_re - 1.0) * lr + ab_im * li) / den
    z_im = (ab_im * lr - (ab_re - 1.0) * li) / den
    br = bre_ref[...]
    bi = bim_ref[...]
    are_ref[...] = ab_re
    aim_ref[...] = ab_im
    bbr_ref[...] = z_re * br - z_im * bi
    bbi_ref[...] = z_re * bi + z_im * br


def _s5_discretise(lam_re, lam_im, log_dt, b_re, b_im):
    n = S5_GROUPS * S5_STATE
    col = lambda a: a.reshape(n, 1)
    ldt = jnp.repeat(log_dt, S5_STATE).reshape(n, 1)
    vec = jax.ShapeDtypeStruct((n, 1), F32)
    mat = jax.ShapeDtypeStruct((n, S5_GROUP), F32)
    return pl.pallas_call(_s5_disc_kernel, out_shape=[vec, vec, mat, mat])(
        col(lam_re), col(lam_im), ldt, b_re.reshape(n, S5_GROUP), b_im.reshape(n, S5_GROUP))


def _s5_kernel(h_ref, g_ref, wbu_ref, are_ref, aim_ref, wcr_ref, wci_ref, d_ref, y_ref,
               st_ref, hs_ref, *, tm, bsz):
    half = 8 * S5_STATE
    nl = half // 128
    nt = tm // bsz

    @pl.when(pl.program_id(0) == 0)
    def _():
        st_ref[...] = jnp.zeros_like(st_ref)

    hn = _rms(pltpu.einshape("btd->tbd", h_ref[...]).reshape(tm, D_MODEL), g_ref[...])
    u = hn.astype(BF16)
    ys = []
    for j in range(S5_BLOCKS):
        buf = hs_ref.at[j % 2]
        buf[...] = _dot(u[:, j * 128:(j + 1) * 128], wbu_ref[j])
        ar = [jnp.broadcast_to(are_ref[j, :, c * 128:(c + 1) * 128], (bsz, 128)) for c in range(nl)]
        ai = [jnp.broadcast_to(aim_ref[j, :, c * 128:(c + 1) * 128], (bsz, 128)) for c in range(nl)]
        state = [st_ref[:, j * 2 * half + c * 128:j * 2 * half + (c + 1) * 128] for c in range(2 * nl)]
        for t in range(nt):
            rows = slice(t * bsz, (t + 1) * bsz)
            for c in range(nl):
                re_l = slice(c * 128, (c + 1) * 128)
                im_l = slice(half + c * 128, half + (c + 1) * 128)
                hr, hi = state[c], state[nl + c]
                nr = ar[c] * hr - ai[c] * hi + buf[rows, re_l]
                ni = ar[c] * hi + ai[c] * hr + buf[rows, im_l]
                buf[rows, re_l] = nr
                buf[rows, im_l] = ni
                state[c], state[nl + c] = nr, ni
        for c in range(2 * nl):
            st_ref[:, j * 2 * half + c * 128:j * 2 * half + (c + 1) * 128] = state[c]
        hb = buf[...].astype(BF16)
        ys.append(_dot(hb[:, :half], wcr_ref[j]) - _dot(hb[:, half:], wci_ref[j]))
    y = jax.nn.gelu(jnp.concatenate(ys, axis=-1) + d_ref[...] * hn)
    y_ref[...] = pltpu.einshape("tbd->btd", y.reshape(nt, bsz, D_MODEL)).astype(y_ref.dtype)


def _s5_mixer_core(bsz, seq, h, gain, lam_re, lam_im, log_dt, b_re, b_im, c_re, c_im, d_skip):
    d = D_MODEL
    assert bsz == 8, "the S5 scan keeps one batch element per sublane"
    ab_re, ab_im, bb_re, bb_im = _s5_discretise(lam_re, lam_im, log_dt, b_re, b_im)
    nb, gl, p, m = S5_BLOCKS, S5_GROUPS // S5_BLOCKS, S5_STATE, S5_GROUP
    eye = jnp.eye(gl, dtype=F32)

    def bu_weight(bb):
        bb = bb.reshape(nb, gl, p, m).transpose(0, 1, 3, 2)
        return (bb[:, :, :, None, :] * eye[None, :, None, :, None]).reshape(nb, gl * m, gl * p)

    def c_weight(cc):
        cc = cc.reshape(nb, gl, m, p).transpose(0, 1, 3, 2)
        return (cc[:, :, :, None, :] * eye[None, :, None, :, None]).reshape(nb, gl * p, gl * m)

    wbu = jnp.concatenate([bu_weight(bb_re), bu_weight(bb_im)], axis=-1).astype(BF16)
    wcr = c_weight(c_re).astype(BF16)
    wci = c_weight(c_im).astype(BF16)
    a_re = ab_re.reshape(nb, 1, gl * p)
    a_im = ab_im.reshape(nb, 1, gl * p)

    nt = min(seq, 64)
    tm = bsz * nt
    const = lambda shape: pl.BlockSpec(shape, lambda i: (0,) * len(shape))
    return pl.pallas_call(
        functools.partial(_s5_kernel, tm=tm, bsz=bsz),
        grid=(seq // nt,),
        in_specs=[pl.BlockSpec((bsz, nt, d), lambda i: (0, i, 0)), const((1, d)), const(wbu.shape),
                  const(a_re.shape), const(a_im.shape), const(wcr.shape), const(wci.shape),
                  const((1, d))],
        out_specs=pl.BlockSpec((bsz, nt, d), lambda i: (0, i, 0)),
        out_shape=jax.ShapeDtypeStruct((bsz, seq, d), BF16),
        scratch_shapes=[pltpu.VMEM((bsz, 2 * S5_GROUPS * S5_STATE), F32),
                        pltpu.VMEM((2, tm, 2 * gl * p), F32)],
        compiler_params=_cparams("arbitrary"),
    )(h.reshape(bsz, seq, d), gain.reshape(1, d), wbu, a_re, a_im, wcr, wci,
      d_skip.reshape(1, d)).reshape(bsz * seq, d)


def kernel(x, mem, positions, norm_mix, norm_xa, norm_mem, norm_ffn, xa_wq, xa_wk, xa_wv, xa_wo, xa_q_norm, xa_k_norm, ffn_w_up, ffn_conv_w, ffn_conv_b, ffn_w_down, hg_lb_logits, mix_w_in, hg_out_norm, mla_q_a_norm, mla_w_uq, mla_kv_a_norm, mla_w_ukv, mla_qn_nope, mla_qn_rope, mla_kn_nope, mla_kn_rope, mix_w_out, s5_lam_re, s5_lam_im, s5_log_dt, s5_b_re, s5_b_im, s5_c_re, s5_c_im, s5_d, s5_w_glu_a, s5_w_glu_b):
    depth = norm_mix.shape[0]
    ctab, stab = _rope_tables(positions)
    mem_k, mem_v = _memory_kv(mem, norm_mem, xa_wk, xa_wv, xa_k_norm)
    bsz, seq, d = x.shape
    tokens = bsz * seq
    h = x.reshape(tokens, d)
    for layer in range(depth):
        j = layer // 2
        if layer % 2 == 0:
            use_bounded, shift = _score_shift(mla_qn_nope[j], mla_qn_rope[j], mla_kn_nope[j], mla_kn_rope[j])
            hg, q, k, v = _mix_prep(h.reshape(bsz, seq, d), norm_mix[layer], mix_w_in[j], mla_q_a_norm[j],
                                    mla_w_uq[j], mla_kv_a_norm[j], mla_w_ukv[j], mla_qn_nope[j],
                                    mla_qn_rope[j], mla_kn_nope[j], mla_kn_rope[j], shift, ctab, stab)
            o_hg = _hgrn(hg, hg_lb_logits, j, hg_out_norm[j]).reshape(tokens, HG_WIDTH)
            o_mla = lax.cond(use_bounded,
                             functools.partial(_mla_attention, bounded=True),
                             functools.partial(_mla_attention, bounded=False),
                             q, k, v).reshape(tokens, MLA_HEADS * MLA_V)
            h = _post_mixer(_post_proj_kernel, bsz, seq, h, [o_hg, o_mla], [mix_w_out[j].astype(BF16)],
                            norm_xa[layer], xa_wq[layer], xa_q_norm[layer], mem_k[layer], mem_v[layer],
                            xa_wo[layer])
        else:
            y = _s5_mixer_core(bsz, seq, h, norm_mix[layer], s5_lam_re[j], s5_lam_im[j], s5_log_dt[j],
                               s5_b_re[j], s5_b_im[j], s5_c_re[j], s5_c_im[j], s5_d[j])
            h = _post_mixer(_post_glu_kernel, bsz, seq, h, [y],
                            [s5_w_glu_a[j].astype(BF16), s5_w_glu_b[j].astype(BF16)],
                            norm_xa[layer], xa_wq[layer], xa_q_norm[layer], mem_k[layer], mem_v[layer],
                            xa_wo[layer])
        h = _conv_ffn(bsz, seq, h, norm_ffn[layer], ffn_w_up[layer], ffn_conv_w[layer],
                      ffn_conv_b[layer], ffn_w_down[layer])
    return h.reshape(bsz, seq, d)
```
